```python
import jax, jax.numpy as jnp
from jax import lax
import numpy as np

D_MODEL = 4096
BATCH = 4
SEQ = 2048
DEPTH = 4
DEC_BATCH = 128
DEC_SEQ = 8
PAST_LEN = 16384
PAGE_SIZE = 128

D_A = D_MODEL // 2
A_GROUPS = 16
A_GD = D_A // A_GROUPS
CHUNK = 128
D_B = D_MODEL // 2
SC_WIDTH = 3
D_INNER = 2 * D_MODEL
M_HEADDIM = 64
M_HEADS = D_INNER // M_HEADDIM
M_STATE = 128
M_GROUPS = 8
M_CONV = 4
CONV_DIM = D_INNER + 2 * M_GROUPS * M_STATE
SSD_CHUNK = 128
D_FF = 4 * D_MODEL
N_EVEN = (DEPTH + 1) // 2
N_ODD = DEPTH // 2
EPS = 1e-5

kernel_name = 'hybrid_gmlp_shortconv_ssd_adaln_step'


def rmsnorm(x, g):
    xf = x.astype(jnp.float32)
    y = xf * lax.rsqrt(jnp.mean(xf * xf, axis=-1, keepdims=True) + EPS)
    return (y * g.astype(jnp.float32)).astype(x.dtype)


def layernorm(x, g, b):
    xf = x.astype(jnp.float32)
    mu = jnp.mean(xf, axis=-1, keepdims=True)
    xc = xf - mu
    var = jnp.mean(xc * xc, axis=-1, keepdims=True)
    return (xc * lax.rsqrt(var + EPS) * g.astype(jnp.float32) + b.astype(jnp.float32)).astype(x.dtype)


def modulate(h, shift, scale):
    return h * (1 + scale) + shift


def causal_dwconv(x, past, w):
    K = w.shape[0]
    L = x.shape[1]
    xp = jnp.concatenate([past.astype(x.dtype), x], axis=1)
    y = xp[:, 0:L] * w[0]
    for k in range(1, K):
        y = y + xp[:, k:k + L] * w[k]
    return y, xp[:, xp.shape[1] - (K - 1):]


def spatial_gate(v, w_s, b_s):
    bsz, L = v.shape[0], v.shape[1]
    mask = jnp.tril(jnp.ones((CHUNK, CHUNK), dtype=bool))
    w = jnp.where(mask[None], w_s, 0)
    if L % CHUNK == 0:
        vc = v.reshape(bsz, L // CHUNK, CHUNK, A_GROUPS, A_GD)
        s = jnp.einsum('gts,bcsgd->bctgd', w, vc) + b_s.T[None, None, :, :, None]
        return s.reshape(bsz, L, A_GROUPS, A_GD)
    return jnp.einsum('gts,bsgd->btgd', w[:, :L, :L], v) + b_s[:, :L].T[None, :, :, None]


def even_mixer(h, sc_past, w_in, ln_g, ln_b, w_s, b_s, sc_w, w_out):
    bsz, L, _ = h.shape
    proj = h @ w_in
    u, v, b_gate, c_gate, x_in = jnp.split(proj, [D_A, 2 * D_A, 2 * D_A + D_B, 2 * D_A + 2 * D_B], axis=-1)
    u = jax.nn.gelu(u)
    v = layernorm(jax.nn.gelu(v), ln_g, ln_b)
    s = spatial_gate(v.reshape(bsz, L, A_GROUPS, A_GD), w_s, b_s).reshape(bsz, L, D_A)
    y_a = u * s
    conv_out, sc_new = causal_dwconv(c_gate * x_in, sc_past, sc_w)
    y_b = b_gate * conv_out
    return jnp.concatenate([y_a, y_b], axis=-1) @ w_out, sc_new, v


def ssd(x, dt, a_log, bmat, cmat, s0, chunk):
    f32 = jnp.float32
    bsz, L = x.shape[0], x.shape[1]
    R = M_HEADS // M_GROUPS
    nc = L // chunk
    A = -jnp.exp(a_log.astype(f32)).reshape(M_GROUPS, R)

    def split(t, *tail):
        return jnp.moveaxis(t.astype(f32).reshape(bsz, nc, chunk, *tail), 1, 0)

    xs = split(x, M_GROUPS, R, M_HEADDIM)
    dts = split(dt, M_GROUPS, R)
    bs = split(bmat, M_GROUPS, M_STATE)
    cs = split(cmat, M_GROUPS, M_STATE)
    tri = jnp.tril(jnp.ones((chunk, chunk), dtype=bool))[None, :, :, None, None]

    def step(S, inp):
        xc, dtc, bc, cc = inp
        cum = jnp.cumsum(dtc * A, axis=1)
        seg = cum[:, :, None] - cum[:, None, :]
        decay = jnp.exp(jnp.where(tri, seg, -jnp.inf))
        cb = jnp.einsum('btgn,bsgn->btsg', cc, bc)
        w = cb[..., None] * decay * dtc[:, None]
        y = jnp.einsum('btsgr,bsgrp->btgrp', w, xc)
        y = y + jnp.einsum('btgn,bgrpn->btgrp', cc, S) * jnp.exp(cum)[..., None]
        tail = jnp.exp(cum[:, -1:] - cum) * dtc
        S_new = S * jnp.exp(cum[:, -1])[..., None, None] + jnp.einsum('bsgr,bsgrp,bsgn->bgrpn', tail, xc, bc)
        return S_new, y

    S0 = s0.astype(f32).reshape(bsz, M_GROUPS, R, M_HEADDIM, M_STATE)
    S, ys = lax.scan(step, S0, (xs, dts, bs, cs))
    y = jnp.moveaxis(ys, 0, 1).reshape(bsz, L, M_HEADS, M_HEADDIM)
    return y, S.reshape(bsz, M_HEADS, M_HEADDIM, M_STATE)


def mamba_mixer(h, conv_past, ssm0, w_in, conv_w, conv_b, dt_bias, a_log, d_skip, norm_g, w_out):
    f32 = jnp.float32
    bsz, L, _ = h.shape
    chunk = SSD_CHUNK if L % SSD_CHUNK == 0 else L
    zxbcdt = h @ w_in
    z, xbc, dt = jnp.split(zxbcdt, [D_INNER, D_INNER + CONV_DIM], axis=-1)
    xbc_c, conv_new = causal_dwconv(xbc, conv_past, conv_w)
    xbc_c = jax.nn.silu(xbc_c + conv_b)
    xs, bmat, cmat = jnp.split(xbc_c, [D_INNER, D_INNER + M_GROUPS * M_STATE], axis=-1)
    xs = xs.reshape(bsz, L, M_HEADS, M_HEADDIM)
    dt = jax.nn.softplus((dt + dt_bias).astype(f32))
    y, S = ssd(xs, dt, a_log, bmat.reshape(bsz, L, M_GROUPS, M_STATE),
               cmat.reshape(bsz, L, M_GROUPS, M_STATE), ssm0, chunk)
    y = y + d_skip.astype(f32)[:, None] * xs.astype(f32)
    y = y.reshape(bsz, L, D_INNER) * jax.nn.silu(z.astype(f32))
    yg = y.reshape(bsz, L, M_GROUPS, D_INNER // M_GROUPS)
    yg = yg * lax.rsqrt(jnp.mean(yg * yg, axis=-1, keepdims=True) + EPS)
    y = (yg.reshape(bsz, L, D_INNER) * norm_g.astype(f32)).astype(h.dtype)
    return y @ w_out, conv_new, S.astype(ssm0.dtype)


def sq_relu_mlp(h, w1, w2):
    a = jax.nn.relu(h @ w1)
    return (a * a) @ w2


def trunk(x, c, sc_state, mc_state, ssm_state, w_ada, b_ada, g_mix, g_ffn,
          ev_w_in, ev_ln_g, ev_ln_b, ev_w_s, ev_b_s, ev_sc_w, ev_w_out,
          m_w_in, m_conv_w, m_conv_b, m_dt_bias, m_a_log, m_d, m_norm_g, m_w_out,
          ff_w1, ff_w2, g_final):
    sc_new, v_new, mc_new, ssm_new = [], [], [], []
    c_act = jax.nn.silu(c)
    for i in range(DEPTH):
        mod = c_act @ w_ada[i] + b_ada[i]
        sh1, sc1, gt1, sh2, sc2, gt2 = [m[:, None, :] for m in jnp.split(mod, 6, axis=-1)]
        hm = modulate(rmsnorm(x, g_mix[i]), sh1, sc1)
        if i % 2 == 0:
            e = i // 2
            out, scb, v = even_mixer(hm, sc_state[e], ev_w_in[e], ev_ln_g[e], ev_ln_b[e],
                                     ev_w_s[e], ev_b_s[e], ev_sc_w[e], ev_w_out[e])
            sc_new.append(scb)
            v_new.append(v)
        else:
            o = i // 2
            out, mcb, S = mamba_mixer(hm, mc_state[o], ssm_state[o], m_w_in[o], m_conv_w[o], m_conv_b[o],
                                      m_dt_bias[o], m_a_log[o], m_d[o], m_norm_g[o], m_w_out[o])
            mc_new.append(mcb)
            ssm_new.append(S)
        x = x + gt1 * out
        hf = modulate(rmsnorm(x, g_ffn[i]), sh2, sc2)
        x = x + gt2 * sq_relu_mlp(hf, ff_w1[i], ff_w2[i])
    return (rmsnorm(x, g_final), jnp.stack(sc_new), jnp.stack(v_new), jnp.stack(mc_new), jnp.stack(ssm_new))


def setup_inputs(seed: int = 0) -> dict:
    key = jax.random.key(seed)
    ks = jax.random.split(key, 40)
    f32 = jnp.float32
    nrm = lambda k, shape, s: jax.random.normal(k, shape, f32) * s
    dt = jnp.exp(jax.random.uniform(ks[20], (N_ODD, M_HEADS), f32) * (np.log(0.1) - np.log(0.001)) + np.log(0.001))
    return {
        'x_prompt': nrm(ks[0], (BATCH, SEQ, D_MODEL), 1.0),
        'x_sample': nrm(ks[1], (DEC_BATCH, DEC_SEQ, D_MODEL), 1.0),
        'state_shortconv': nrm(ks[2], (N_EVEN, DEC_BATCH, SC_WIDTH - 1, D_B), 1.0),
        'state_mamba_conv': nrm(ks[3], (N_ODD, DEC_BATCH, M_CONV - 1, CONV_DIM), 1.0),
        'state_ssm': nrm(ks[4], (N_ODD, DEC_BATCH, M_HEADS, M_HEADDIM, M_STATE), 0.5),
        'c_prompt': nrm(ks[5], (BATCH, D_MODEL), 1.0),
        'c_sample': nrm(ks[6], (DEC_BATCH, D_MODEL), 1.0),
        'w_ada': nrm(ks[7], (DEPTH, D_MODEL, 6 * D_MODEL), 0.5 * D_MODEL ** -0.5),
        'b_ada': nrm(ks[8], (DEPTH, 6 * D_MODEL), 0.01),
        'g_mix': 1.0 + nrm(ks[9], (DEPTH, D_MODEL), 0.02),
        'g_ffn': 1.0 + nrm(ks[10], (DEPTH, D_MODEL), 0.02),
        'ev_w_in': nrm(ks[11], (N_EVEN, D_MODEL, 2 * D_A + 3 * D_B), D_MODEL ** -0.5),
        'ev_ln_g': 1.0 + nrm(ks[12], (N_EVEN, D_A), 0.02),
        'ev_ln_b': nrm(ks[13], (N_EVEN, D_A), 0.02),
        'ev_w_s': nrm(ks[14], (N_EVEN, A_GROUPS, CHUNK, CHUNK), CHUNK ** -0.5),
        'ev_b_s': 1.0 + nrm(ks[15], (N_EVEN, A_GROUPS, CHUNK), 0.01),
        'ev_sc_w': nrm(ks[16], (N_EVEN, SC_WIDTH, D_B), SC_WIDTH ** -0.5),
        'ev_w_out': nrm(ks[17], (N_EVEN, D_A + D_B, D_MODEL), (D_A + D_B) ** -0.5),
        'm_w_in': nrm(ks[18], (N_ODD, D_MODEL, 2 * D_INNER + 2 * M_GROUPS * M_STATE + M_HEADS), D_MODEL ** -0.5),
        'm_conv_w': nrm(ks[19], (N_ODD, M_CONV, CONV_DIM), M_CONV ** -0.5),
        'm_conv_b': nrm(ks[21], (N_ODD, CONV_DIM), 0.02),
        'm_dt_bias': dt + jnp.log(-jnp.expm1(-dt)),
        'm_a_log': jnp.log(jax.random.uniform(ks[22], (N_ODD, M_HEADS), f32, 1.0, 16.0)),
        'm_d': 1.0 + nrm(ks[23], (N_ODD, M_HEADS), 0.1),
        'm_norm_g': 1.0 + nrm(ks[24], (N_ODD, D_INNER), 0.02),
        'm_w_out': nrm(ks[25], (N_ODD, D_INNER, D_MODEL), D_INNER ** -0.5),
        'ff_w1': nrm(ks[26], (DEPTH, D_MODEL, D_FF), D_MODEL ** -0.5),
        'ff_w2': nrm(ks[27], (DEPTH, D_FF, D_MODEL), D_FF ** -0.5),
        'g_final': 1.0 + nrm(ks[28], (D_MODEL,), 0.02),
    }


def reference(x_prompt, x_sample, state_shortconv, state_mamba_conv, state_ssm, c_prompt, c_sample,
              w_ada, b_ada, g_mix, g_ffn, ev_w_in, ev_ln_g, ev_ln_b, ev_w_s, ev_b_s, ev_sc_w, ev_w_out,
              m_w_in, m_conv_w, m_conv_b, m_dt_bias, m_a_log, m_d, m_norm_g, m_w_out,
              ff_w1, ff_w2, g_final):
    weights = (w_ada, b_ada, g_mix, g_ffn, ev_w_in, ev_ln_g, ev_ln_b, ev_w_s, ev_b_s, ev_sc_w, ev_w_out,
               m_w_in, m_conv_w, m_conv_b, m_dt_bias, m_a_log, m_d, m_norm_g, m_w_out,
               ff_w1, ff_w2, g_final)
    bp = x_prompt.shape[0]
    dtp = x_prompt.dtype
    sc0 = jnp.zeros((N_EVEN, bp, SC_WIDTH - 1, D_B), dtp)
    mc0 = jnp.zeros((N_ODD, bp, M_CONV - 1, CONV_DIM), dtp)
    ssm0 = jnp.zeros((N_ODD, bp, M_HEADS, M_HEADDIM, M_STATE), dtp)
    y_prompt, sc_p, _, mc_p, ssm_p = trunk(x_prompt, c_prompt, sc0, mc0, ssm0, *weights)
    y_sample, sc_s, v_s, mc_s, ssm_s = trunk(x_sample, c_sample, state_shortconv, state_mamba_conv,
                                            state_ssm, *weights)
    return (y_prompt, y_sample, sc_p, sc_s, v_s, mc_p, mc_s, ssm_p, ssm_s)
```

```python
import functools

import jax
import jax.numpy as jnp
from jax import lax
from jax.experimental import pallas as pl
from jax.experimental.pallas import tpu as pltpu

EPS = 1e-5
ROW_BLOCK = 128
SUBLANES = 8
V7X_VMEM_LIMIT = 58 * 1024 * 1024
BF16 = jnp.bfloat16
F32 = jnp.float32


def _params(n_axes, vmem_mb=None):
    limit = V7X_VMEM_LIMIT if vmem_mb is None else min(vmem_mb * 1024 * 1024, V7X_VMEM_LIMIT)
    return pltpu.CompilerParams(dimension_semantics=("arbitrary",) * n_axes, vmem_limit_bytes=limit)


def _pick_block(n, cap, *also):
    for c in (2048, 1024, 512, 256, 128):
        if c <= cap and n % c == 0 and all(a % c == 0 for a in also):
            return c
    raise ValueError(f"no lane-aligned block divides {n}")


def _dot(a, b):
    return jnp.dot(a, b, preferred_element_type=F32)


def _dot_nt(a, b):
    return lax.dot_general(a, b, (((1,), (1,)), ((), ())), preferred_element_type=F32)


def _ada_kernel(c_ref, w_ref, b_ref, o_ref, cact_ref, *, n_prompt, n_dec):
    first = jnp.logical_and(pl.program_id(0) == 0, pl.program_id(1) == 0)

    @pl.when(first)
    def _():
        c = c_ref[...]
        cact_ref[...] = (c * jax.nn.sigmoid(c)).astype(BF16)

    acc = _dot(cact_ref[...], w_ref[...].astype(BF16)) + b_ref[...]
    for b in range(n_prompt):
        row = acc[n_dec + b:n_dec + b + 1, :]
        o_ref[b * ROW_BLOCK:(b + 1) * ROW_BLOCK, :] = jnp.broadcast_to(row, (ROW_BLOCK, acc.shape[1]))
    o_ref[n_prompt * ROW_BLOCK:, :] = acc[:n_dec, :]


def _ada_all(c_prompt, c_sample, w_ada, b_ada, bn=512):
    depth, d, n6 = w_ada.shape
    n_prompt, n_dec = c_prompt.shape[0], c_sample.shape[0]
    assert n_dec == ROW_BLOCK and n6 % bn == 0
    pad = (-(n_dec + n_prompt)) % SUBLANES
    c_rows = jnp.concatenate([c_sample, c_prompt, jnp.zeros((pad, d), c_prompt.dtype)], axis=0)
    rows = c_rows.shape[0]
    n_groups = n_prompt + 1
    return pl.pallas_call(
        functools.partial(_ada_kernel, n_prompt=n_prompt, n_dec=n_dec),
        out_shape=jax.ShapeDtypeStruct((depth, n_groups * ROW_BLOCK, n6), F32),
        grid=(depth, n6 // bn),
        in_specs=[
            pl.BlockSpec((rows, d), lambda l, n: (0, 0)),
            pl.BlockSpec((None, d, bn), lambda l, n: (l, 0, n)),
            pl.BlockSpec((None, 1, bn), lambda l, n: (l, 0, n)),
        ],
        out_specs=pl.BlockSpec((None, n_groups * ROW_BLOCK, bn), lambda l, n: (l, 0, n)),
        scratch_shapes=[pltpu.VMEM((rows, d), BF16)],
        compiler_params=_params(2),
        name="ada_mod",
    )(c_rows, w_ada, b_ada.reshape(depth, 1, n6))


class _Rows:
    def __init__(self, n_prompt, seq, n_dec, dec_seq):
        self.n_prompt, self.seq, self.n_dec, self.dec_seq = n_prompt, seq, n_dec, dec_seq
        self.tp = n_prompt * seq
        self.ts = n_dec * dec_seq
        self.t = self.tp + self.ts
        assert n_dec == ROW_BLOCK and seq % ROW_BLOCK == 0
        self.chunks_per_seq = seq // ROW_BLOCK
        self.prompt_blocks = self.tp // ROW_BLOCK

    def tile_rows(self, cap):
        bm = cap
        while self.seq % bm or self.ts % bm:
            bm //= 2
        assert bm >= ROW_BLOCK
        return bm

    def group_of_tile(self, i, bm):
        return jnp.where(i * bm < self.tp, (i * bm) // self.seq, self.n_prompt)


def _norm_kernel(*refs, modulated):
    if modulated:
        x_ref, g_ref, sh_ref, sc_ref, o_ref = refs
    else:
        x_ref, g_ref, o_ref = refs
    x = x_ref[...]
    y = x * lax.rsqrt(jnp.mean(x * x, axis=-1, keepdims=True) + EPS) * g_ref[...]
    if modulated:
        bm, d = y.shape
        y3 = y.reshape(bm // ROW_BLOCK, ROW_BLOCK, d)
        y = (y3 * (1.0 + sc_ref[...])[None] + sh_ref[...][None]).reshape(bm, d)
    o_ref[...] = y.astype(o_ref.dtype)


def _norm_mod(x, g, rows, mods=None, layer=0, shift_col=0, scale_col=0, out_dtype=BF16, name="norm"):
    t, d = x.shape
    bm = rows.tile_rows(512)
    in_specs = [pl.BlockSpec((bm, d), lambda i: (i, 0)), pl.BlockSpec((1, d), lambda i: (0, 0))]
    args = [x, g.reshape(1, d)]
    if mods is not None:
        for col in (shift_col, scale_col):
            in_specs.append(pl.BlockSpec((None, ROW_BLOCK, d),
                                         lambda i, col=col: (layer, rows.group_of_tile(i, bm), col)))
            args.append(mods)
    return pl.pallas_call(
        functools.partial(_norm_kernel, modulated=mods is not None),
        out_shape=jax.ShapeDtypeStruct((t, d), out_dtype),
        grid=(t // bm,),
        in_specs=in_specs,
        out_specs=pl.BlockSpec((bm, d), lambda i: (i, 0)),
        compiler_params=_params(1),
        name=name,
    )(*args)


def _mm_kernel(*refs, nk, epi):
    if epi == "resid":
        x_ref, w_ref, res_ref, gate_ref, o_ref, *scratch = refs
    else:
        x_ref, w_ref, o_ref, *scratch = refs

    def epilogue(acc):
        if epi == "f32":
            o_ref[...] = acc
        elif epi == "relu2":
            a = jnp.maximum(acc, 0.0)
            o_ref[...] = (a * a).astype(o_ref.dtype)
        else:
            bm, bn = acc.shape
            gated = acc.reshape(bm // ROW_BLOCK, ROW_BLOCK, bn) * gate_ref[...][None]
            o_ref[...] = res_ref[...] + gated.reshape(bm, bn)

    if nk == 1:
        (wb_ref,) = scratch

        @pl.when(pl.program_id(1) == 0)
        def _():
            wb_ref[...] = w_ref[...].astype(BF16)

        epilogue(_dot(x_ref[...], wb_ref[...]))
    else:
        (acc_ref,) = scratch
        k = pl.program_id(2)

        @pl.when(k == 0)
        def _():
            acc_ref[...] = jnp.zeros_like(acc_ref)

        acc_ref[...] += _dot(x_ref[...], w_ref[...].astype(BF16))

        @pl.when(k == nk - 1)
        def _():
            epilogue(acc_ref[...])


def _matmul(x, w, layer, rows, *, n_cols, col0=0, bm, bn_cap, bk=None, epi="f32", out_dtype=F32,
            res=None, mods=None, mod_layer=0, gate_col=0, name="mm"):
    t, kdim = x.shape
    bk = kdim if bk is None else bk
    bn = _pick_block(n_cols, bn_cap, col0, gate_col)
    assert t % bm == 0 and kdim % bk == 0
    nk = kdim // bk
    in_specs = [
        pl.BlockSpec((bm, bk), lambda n, m, k: (m, k)),
        pl.BlockSpec((None, bk, bn), lambda n, m, k: (layer, k, col0 // bn + n)),
    ]
    args = [x, w]
    if epi == "resid":
        assert gate_col % bn == 0
        in_specs.append(pl.BlockSpec((bm, bn), lambda n, m, k: (m, n)))
        in_specs.append(pl.BlockSpec((None, ROW_BLOCK, bn),
                                     lambda n, m, k: (mod_layer, rows.group_of_tile(m, bm), gate_col // bn + n)))
        args += [res, mods]
    scratch = [pltpu.VMEM((bk, bn), BF16)] if nk == 1 else [pltpu.VMEM((bm, bn), F32)]
    return pl.pallas_call(
        functools.partial(_mm_kernel, nk=nk, epi=epi),
        out_shape=jax.ShapeDtypeStruct((t, n_cols), out_dtype),
        grid=(n_cols // bn, t // bm, nk),
        in_specs=in_specs,
        out_specs=pl.BlockSpec((bm, bn), lambda n, m, k: (m, n)),
        scratch_shapes=scratch,
        compiler_params=_params(3),
        name=name,
    )(*args)


def _gelu_layernorm(v, g, b):
    v = jax.nn.gelu(v)
    mu = jnp.mean(v, axis=-1, keepdims=True)
    vc = v - mu
    var = jnp.mean(vc * vc, axis=-1, keepdims=True)
    return vc * lax.rsqrt(var + EPS) * g + b


def _even_prompt_kernel(u_ref, v_ref, bg_ref, cg_ref, xi_ref, lng_ref, lnb_ref, ws_ref, bse_ref, scw_ref,
                        y_ref, tail_ref, carry_ref, *, chunks_per_seq, n_groups, gd, d_a, taps):
    j = pl.program_id(0)

    @pl.when(j % chunks_per_seq == 0)
    def _():
        carry_ref[...] = jnp.zeros_like(carry_ref)

    u = jax.nn.gelu(u_ref[...])
    vn = _gelu_layernorm(v_ref[...], lng_ref[...], lnb_ref[...])
    tri = (lax.broadcasted_iota(jnp.int32, (ROW_BLOCK, ROW_BLOCK), 0)
           >= lax.broadcasted_iota(jnp.int32, (ROW_BLOCK, ROW_BLOCK), 1))
    for g in range(n_groups):
        sl = slice(g * gd, (g + 1) * gd)
        wg = jnp.where(tri, ws_ref[g], 0.0).astype(BF16)
        s = _dot(wg, vn[:, sl].astype(BF16)) + bse_ref[:, sl]
        y_ref[:, sl] = (u[:, sl] * s).astype(y_ref.dtype)

    cx = cg_ref[...] * xi_ref[...]
    row = lax.broadcasted_iota(jnp.int32, cx.shape, 0)
    conv = cx * scw_ref[taps - 1:taps, :]
    for d in range(1, taps):
        shifted = pltpu.roll(cx, d, axis=0)
        for r in range(d):
            src = SUBLANES - d + r
            shifted = jnp.where(row == r, carry_ref[src:src + 1, :], shifted)
        conv = conv + shifted * scw_ref[taps - 1 - d:taps - d, :]
    y_ref[:, d_a:] = (bg_ref[...] * conv).astype(y_ref.dtype)
    last = cx[ROW_BLOCK - SUBLANES:, :]
    carry_ref[...] = last
    tail_ref[...] = last


def _even_sample_kernel(wsm_ref, bsm_ref, u_ref, v_ref, bg_ref, cg_ref, xi_ref, *rest,
                        n_steps, n_groups, gd, d_a, taps):
    st_refs = rest[:taps - 1]
    lng_ref, lnb_ref, scw_ref, _, y_ref, vout_ref, scn_ref, vbuf_ref, car_ref = rest[taps - 1:]
    t = pl.program_id(0)

    @pl.when(t == 0)
    def _():
        vbuf_ref[...] = jnp.zeros_like(vbuf_ref)
        for k in range(taps - 1):
            car_ref[k] = st_refs[k][...]

    u = jax.nn.gelu(u_ref[...])
    vn = _gelu_layernorm(v_ref[...], lng_ref[...], lnb_ref[...])
    vout_ref[...] = vn
    vbuf_ref[t] = vn
    for g in range(n_groups):
        sl = slice(g * gd, (g + 1) * gd)
        acc = jnp.zeros((ROW_BLOCK, gd), F32)
        for s in range(n_steps):
            coef = jnp.where(s <= t, wsm_ref[(g * n_steps + t) * n_steps + s], 0.0)
            acc = acc + coef * vbuf_ref[s, :, sl]
        sg = acc + bsm_ref[g * n_steps + t]
        y_ref[:, sl] = (u[:, sl] * sg).astype(y_ref.dtype)

    cx = cg_ref[...] * xi_ref[...]
    conv = cx * scw_ref[taps - 1:taps, :]
    for k in range(taps - 1):
        conv = conv + car_ref[k] * scw_ref[k:k + 1, :]
    y_ref[:, d_a:] = (bg_ref[...] * conv).astype(y_ref.dtype)
    scn_ref[...] = cx
    for k in range(taps - 2):
        car_ref[k] = car_ref[k + 1]
    car_ref[taps - 2] = cx


def _even_mixer(proj, rows, sc_state, ln_g, ln_b, w_s, b_s, sc_w):
    n_groups, chunk, _ = w_s.shape
    d_a = ln_g.shape[0]
    taps, d_b = sc_w.shape
    gd = d_a // n_groups
    assert chunk == ROW_BLOCK and d_a == d_b and gd % 128 == 0 and taps - 1 <= SUBLANES
    bw = d_a
    t, n_dec, n_steps = rows.t, rows.n_dec, rows.dec_seq
    assert n_steps <= chunk and n_steps >= taps - 1
    col_specs = lambda row_of: [pl.BlockSpec((ROW_BLOCK, bw), lambda j, c=c: (row_of(j), c)) for c in range(5)]
    const = lambda shape: pl.BlockSpec(shape, lambda j: (0,) * len(shape))

    bse = jnp.repeat(b_s.T, gd, axis=1)
    y, tail = pl.pallas_call(
        functools.partial(_even_prompt_kernel, chunks_per_seq=rows.chunks_per_seq, n_groups=n_groups,
                          gd=gd, d_a=d_a, taps=taps),
        out_shape=(jax.ShapeDtypeStruct((t, d_a + d_b), BF16),
                   jax.ShapeDtypeStruct((rows.n_prompt, SUBLANES, d_b), F32)),
        grid=(rows.prompt_blocks,),
        in_specs=col_specs(lambda j: j) + [
            const((1, d_a)), const((1, d_a)), const((n_groups, chunk, chunk)), const((chunk, d_a)),
            const((taps, d_b))],
        out_specs=(pl.BlockSpec((ROW_BLOCK, d_a + d_b), lambda j: (j, 0)),
                   pl.BlockSpec((None, SUBLANES, d_b), lambda j: (j // rows.chunks_per_seq, 0, 0))),
        scratch_shapes=[pltpu.VMEM((SUBLANES, d_b), F32)],
        compiler_params=_params(1),
        name="even_mixer_prompt",
    )(proj, proj, proj, proj, proj, ln_g.reshape(1, d_a), ln_b.reshape(1, d_a), w_s, bse, sc_w)

    smem = pl.BlockSpec(memory_space=pltpu.SMEM)
    state2d = sc_state.reshape(n_dec, (taps - 1) * d_b)
    first_new = n_steps - (taps - 1)
    y, v_new, sc_new = pl.pallas_call(
        functools.partial(_even_sample_kernel, n_steps=n_steps, n_groups=n_groups, gd=gd, d_a=d_a, taps=taps),
        out_shape=(jax.ShapeDtypeStruct((t, d_a + d_b), BF16),
                   jax.ShapeDtypeStruct((n_dec, n_steps * d_a), F32),
                   jax.ShapeDtypeStruct((n_dec, (taps - 1) * d_b), F32)),
        grid=(n_steps,),
        in_specs=[smem, smem] + col_specs(lambda j: rows.prompt_blocks + j)
        + [pl.BlockSpec((n_dec, d_b), lambda j, k=k: (0, k)) for k in range(taps - 1)]
        + [const((1, d_a)), const((1, d_a)), const((taps, d_b)), pl.BlockSpec(memory_space=pl.ANY)],
        out_specs=(pl.BlockSpec((ROW_BLOCK, d_a + d_b), lambda j: (rows.prompt_blocks + j, 0)),
                   pl.BlockSpec((n_dec, d_a), lambda j: (0, j)),
                   pl.BlockSpec((n_dec, d_b), lambda j: (0, jnp.maximum(j - first_new, 0)))),
        scratch_shapes=[pltpu.VMEM((n_steps, n_dec, d_a), F32), pltpu.VMEM((taps - 1, n_dec, d_b), F32)],
        input_output_aliases={9 + taps: 0},
        compiler_params=_params(1),
        name="even_mixer_decode",
    )(w_s[:, :n_steps, :n_steps].reshape(-1), b_s[:, :n_steps].reshape(-1), proj, proj, proj, proj, proj,
      *([state2d] * (taps - 1)), ln_g.reshape(1, d_a), ln_b.reshape(1, d_a), sc_w, y)
    return (y, tail[:, SUBLANES - (taps - 1):, :], sc_new.reshape(n_dec, taps - 1, d_b),
            v_new.reshape(n_dec, n_steps, d_a))


def _silu(x):
    return x * jax.nn.sigmoid(x)


def _mpre_prompt_kernel(x_ref, w_ref, b_ref, o_ref, tail_ref, carry_ref, *, chunks_per_seq, taps):
    j = pl.program_id(1)

    @pl.when(j % chunks_per_seq == 0)
    def _():
        carry_ref[...] = jnp.zeros_like(carry_ref)

    x = x_ref[...]
    row = lax.broadcasted_iota(jnp.int32, x.shape, 0)
    acc = x * w_ref[taps - 1:taps, :]
    for d in range(1, taps):
        shifted = pltpu.roll(x, d, axis=0)
        for r in range(d):
            src = SUBLANES - d + r
            shifted = jnp.where(row == r, carry_ref[src:src + 1, :], shifted)
        acc = acc + shifted * w_ref[taps - 1 - d:taps - d, :]
    o_ref[...] = _silu(acc + b_ref[...])
    last = x[ROW_BLOCK - SUBLANES:, :]
    carry_ref[...] = last
    tail_ref[...] = last


def _mpre_sample_kernel(x_ref, *rest, taps):
    st_refs = rest[:taps - 1]
    w_ref, b_ref, o_ref, new_ref, car_ref = rest[taps - 1:]
    t = pl.program_id(1)

    @pl.when(t == 0)
    def _():
        for k in range(taps - 1):
            car_ref[k] = st_refs[k][...]

    x = x_ref[...]
    acc = x * w_ref[taps - 1:taps, :]
    for k in range(taps - 1):
        acc = acc + car_ref[k] * w_ref[k:k + 1, :]
    o_ref[...] = _silu(acc + b_ref[...])
    new_ref[...] = x
    for k in range(taps - 2):
        car_ref[k] = car_ref[k + 1]
    car_ref[taps - 2] = x


def _mamba_pre(zx, rows, xbc_col0, conv_state, conv_w, conv_b):
    taps, conv_dim = conv_w.shape
    bw = _pick_block(conv_dim, 2048, xbc_col0)
    assert taps - 1 <= SUBLANES
    ncb = conv_dim // bw
    cb0 = xbc_col0 // bw
    n_dec, n_steps = rows.n_dec, rows.dec_seq
    wspec = pl.BlockSpec((taps, bw), lambda c, j: (0, c))
    bspec = pl.BlockSpec((1, bw), lambda c, j: (0, c))
    xp, tail = pl.pallas_call(
        functools.partial(_mpre_prompt_kernel, chunks_per_seq=rows.chunks_per_seq, taps=taps),
        out_shape=(jax.ShapeDtypeStruct((rows.tp, conv_dim), F32),
                   jax.ShapeDtypeStruct((rows.n_prompt, SUBLANES, conv_dim), F32)),
        grid=(ncb, rows.prompt_blocks),
        in_specs=[pl.BlockSpec((ROW_BLOCK, bw), lambda c, j: (j, cb0 + c)), wspec, bspec],
        out_specs=(pl.BlockSpec((ROW_BLOCK, bw), lambda c, j: (j, c)),
                   pl.BlockSpec((None, SUBLANES, bw), lambda c, j: (j // rows.chunks_per_seq, 0, c))),
        scratch_shapes=[pltpu.VMEM((SUBLANES, bw), F32)],
        compiler_params=_params(2),
        name="mamba_conv_prompt",
    )(zx, conv_w, conv_b.reshape(1, conv_dim))

    state2d = conv_state.reshape(n_dec, (taps - 1) * conv_dim)
    first_new = n_steps - (taps - 1)
    xs, new = pl.pallas_call(
        functools.partial(_mpre_sample_kernel, taps=taps),
        out_shape=(jax.ShapeDtypeStruct((n_dec, n_steps * conv_dim), F32),
                   jax.ShapeDtypeStruct((n_dec, (taps - 1) * conv_dim), F32)),
        grid=(ncb, n_steps),
        in_specs=[pl.BlockSpec((ROW_BLOCK, bw), lambda c, j: (rows.prompt_blocks + j, cb0 + c))]
        + [pl.BlockSpec((n_dec, bw), lambda c, j, k=k: (0, k * ncb + c)) for k in range(taps - 1)]
        + [wspec, bspec],
        out_specs=(pl.BlockSpec((n_dec, bw), lambda c, j: (0, j * ncb + c)),
                   pl.BlockSpec((n_dec, bw), lambda c, j: (0, jnp.maximum(j - first_new, 0) * ncb + c))),
        scratch_shapes=[pltpu.VMEM((taps - 1, n_dec, bw), F32)],
        compiler_params=_params(2),
        name="mamba_conv_decode",
    )(zx, *([state2d] * (taps - 1)), conv_w, conv_b.reshape(1, conv_dim))
    return (xp, xs.reshape(n_dec * n_steps, conv_dim), tail[:, SUBLANES - (taps - 1):, :],
            new.reshape(n_dec, taps - 1, conv_dim))


def _copy_kernel(x_ref, o_ref):
    o_ref[...] = x_ref[...].astype(o_ref.dtype)


def _steps_to_batch_major(x, rows, col0, n_cols, bw):
    assert n_cols % bw == 0 and col0 % bw == 0
    nc = n_cols // bw
    out = pl.pallas_call(
        _copy_kernel,
        out_shape=jax.ShapeDtypeStruct((rows.n_dec, rows.dec_seq * n_cols), F32),
        grid=(rows.dec_seq, nc),
        in_specs=[pl.BlockSpec((ROW_BLOCK, bw), lambda s, c: (rows.prompt_blocks + s, col0 // bw + c))],
        out_specs=pl.BlockSpec((rows.n_dec, bw), lambda s, c: (0, s * nc + c)),
        compiler_params=_params(2),
        name="to_batch_major",
    )(x)
    return out.reshape(rows.n_dec * rows.dec_seq, n_cols)


def _batch_to_steps_major(y_bt, into, rows, bw):
    c = y_bt.shape[1]
    assert c % bw == 0
    nc = c // bw
    return pl.pallas_call(
        lambda x_ref, _, o_ref: _copy_kernel(x_ref, o_ref),
        out_shape=jax.ShapeDtypeStruct(into.shape, into.dtype),
        grid=(rows.dec_seq, nc),
        in_specs=[pl.BlockSpec((rows.n_dec, bw), lambda s, j: (0, s * nc + j)),
                  pl.BlockSpec(memory_space=pl.ANY)],
        out_specs=pl.BlockSpec((ROW_BLOCK, bw), lambda s, j: (rows.prompt_blocks + s, j)),
        input_output_aliases={1: 0},
        compiler_params=_params(2),
        name="to_steps_major",
    )(y_bt.reshape(rows.n_dec, rows.dec_seq * c), into)


def _ssd_kernel(x_ref, b_ref, c_ref, dt_ref, z_ref, s0_ref, dtb_ref, alog_ref, dsk_ref, ng_ref,
                y_ref, s_ref, xs_ref, yb_ref, *, n_in, heads, hd, ns, groups):
    L = ROW_BLOCK
    hpg = heads // groups
    gw = hpg * hd

    @pl.when(pl.program_id(1) == 0)
    def _():
        s_ref[...] = s0_ref[...]

    def padded(a):
        if n_in == L:
            return a
        return jnp.concatenate([a, jnp.zeros((L - n_in, a.shape[1]), a.dtype)], axis=0)

    dt = padded(jax.nn.softplus(dt_ref[...] + dtb_ref[...]))
    a_neg = -jnp.exp(alog_ref[...])
    cum = dt * a_neg
    row = lax.broadcasted_iota(jnp.int32, cum.shape, 0)
    shift = 1
    while shift < L:
        cum = cum + jnp.where(row >= shift, pltpu.roll(cum, shift, axis=0), 0.0)
        shift *= 2
    cum_t = cum.T
    dt_t = dt.T
    tri = (lax.broadcasted_iota(jnp.int32, (L, L), 0) >= lax.broadcasted_iota(jnp.int32, (L, L), 1))

    for g in range(groups):
        bg = padded(b_ref[:, g * ns:(g + 1) * ns]).astype(BF16)
        cg = padded(c_ref[:, g * ns:(g + 1) * ns]).astype(BF16)
        cb = jnp.where(tri, _dot_nt(cg, bg), 0.0)
        s_g = s_ref[g * hpg:(g + 1) * hpg].reshape(gw, ns)
        cs = _dot_nt(cg, s_g.astype(BF16))
        state_decay = []
        for r in range(hpg):
            h = g * hpg + r
            ccol = jnp.broadcast_to(cum[:, h:h + 1], (L, L))
            dcol = jnp.broadcast_to(dt[:, h:h + 1], (L, hd))
            seg = jnp.minimum(ccol - cum_t[h:h + 1, :], 0.0)
            w = cb * jnp.exp(seg) * dt_t[h:h + 1, :]
            xh = padded(x_ref[:, h * hd:(h + 1) * hd])
            ecol = jnp.exp(ccol[:, :hd])
            yb_ref[:, r * hd:(r + 1) * hd] = _dot(w.astype(BF16), xh.astype(BF16)) + cs[:, r * hd:(r + 1) * hd] * ecol
            clast = ccol[L - 1:L, :hd]
            xs_ref[:, r * hd:(r + 1) * hd] = xh * (jnp.exp(clast - ccol[:, :hd]) * dcol)
            state_decay.append(jnp.exp(ccol[L - 1:L, :ns]))
        upd = _dot(xs_ref[...].T.astype(BF16), bg)
        for r in range(hpg):
            h = g * hpg + r
            s_ref[h] = s_g[r * hd:(r + 1) * hd, :] * state_decay[r] + upd[r * hd:(r + 1) * hd, :]

        sl = slice(g * gw, (g + 1) * gw)
        xg = x_ref[:, sl]
        zg = z_ref[:, sl]
        yg = (yb_ref[:n_in, :] + dsk_ref[:, sl] * xg) * _silu(zg)
        yg = yg * lax.rsqrt(jnp.mean(yg * yg, axis=-1, keepdims=True) + EPS)
        y_ref[:, sl] = (yg * ng_ref[:, sl]).astype(y_ref.dtype)


def _ssd(x, xcol_w, bcol, ccol, dtraw, z, s0, dt_bias, a_log, d_skip, norm_g, *, n_seq, n_chunks, n_in,
         groups, out_dtype, name):
    _, heads, hd, ns = s0.shape
    d_inner = heads * hd
    gn = groups * ns
    r = n_seq * n_chunks * n_in
    row = lambda b, c: b * n_chunks + c
    const = lambda width: pl.BlockSpec((1, width), lambda b, c: (0, 0))
    return pl.pallas_call(
        functools.partial(_ssd_kernel, n_in=n_in, heads=heads, hd=hd, ns=ns, groups=groups),
        out_shape=(jax.ShapeDtypeStruct((r, d_inner), out_dtype),
                   jax.ShapeDtypeStruct(s0.shape, F32)),
        grid=(n_seq, n_chunks),
        in_specs=[
            pl.BlockSpec((n_in, d_inner), lambda b, c: (row(b, c), 0)),
            pl.BlockSpec((n_in, gn), lambda b, c: (row(b, c), bcol)),
            pl.BlockSpec((n_in, gn), lambda b, c: (row(b, c), ccol)),
            pl.BlockSpec((n_in, heads), lambda b, c: (row(b, c), 0)),
            pl.BlockSpec((n_in, d_inner), lambda b, c: (row(b, c), 0)),
            pl.BlockSpec((None, heads, hd, ns), lambda b, c: (b, 0, 0, 0)),
            const(heads), const(heads), const(d_inner), const(d_inner),
        ],
        out_specs=(pl.BlockSpec((n_in, d_inner), lambda b, c: (row(b, c), 0)),
                   pl.BlockSpec((None, heads, hd, ns), lambda b, c: (b, 0, 0, 0))),
        scratch_shapes=[pltpu.VMEM((ROW_BLOCK, d_inner // groups), F32),
                        pltpu.VMEM((ROW_BLOCK, d_inner // groups), F32)],
        compiler_params=_params(2),
        name=name,
    )(x, x, x, dtraw, z, s0, dt_bias.reshape(1, heads), a_log.reshape(1, heads),
      jnp.repeat(d_skip, hd).reshape(1, d_inner), norm_g.reshape(1, d_inner))


def kernel(x_prompt, x_sample, state_shortconv, state_mamba_conv, state_ssm, c_prompt, c_sample, w_ada, b_ada, g_mix, g_ffn, ev_w_in, ev_ln_g, ev_ln_b, ev_w_s, ev_b_s, ev_sc_w, ev_w_out, m_w_in, m_conv_w, m_conv_b, m_dt_bias, m_a_log, m_d, m_norm_g, m_w_out, ff_w1, ff_w2, g_final):
    n_prompt, seq, d = x_prompt.shape
    n_dec, dec_seq, _ = x_sample.shape
    rows = _Rows(n_prompt, seq, n_dec, dec_seq)
    depth = w_ada.shape[0]
    d_ff = ff_w1.shape[2]
    d_a = ev_ln_g.shape[1]
    d_b = ev_sc_w.shape[2]
    heads = m_a_log.shape[1]
    d_inner = m_norm_g.shape[1]
    conv_dim = m_conv_w.shape[2]
    ns = state_ssm.shape[-1]
    groups = (conv_dim - d_inner) // (2 * ns)
    gn = groups * ns
    assert d_inner % gn == 0 and m_w_in.shape[2] == 2 * d_inner + 2 * gn + heads

    bm = rows.tile_rows(512)
    bm_k = rows.tile_rows(1024)
    lane_block = _pick_block(d_inner, 2048)

    x = jnp.concatenate([x_prompt.reshape(rows.tp, d),
                         jnp.swapaxes(x_sample, 0, 1).reshape(rows.ts, d)], axis=0)
    mods = _ada_all(c_prompt, c_sample, w_ada, b_ada)

    sc_p, sc_s, v_s, mc_p, mc_s, ssm_p, ssm_s = [], [], [], [], [], [], []
    for i in range(depth):
        hm = _norm_mod(x, g_mix[i], rows, mods, i, shift_col=0, scale_col=1, name="norm_mix")
        if i % 2 == 0:
            e = i // 2
            n_proj = ev_w_in.shape[2]
            proj = _matmul(hm, ev_w_in, e, rows, n_cols=n_proj, bm=bm, bn_cap=1024, name="even_in")
            ymix, scp, scs, vs = _even_mixer(proj, rows, state_shortconv[e], ev_ln_g[e], ev_ln_b[e],
                                             ev_w_s[e], ev_b_s[e], ev_sc_w[e])
            sc_p.append(scp)
            sc_s.append(scs)
            v_s.append(vs)
            x = _matmul(ymix, ev_w_out, e, rows, n_cols=d, bm=bm, bn_cap=512, epi="resid", res=x, mods=mods,
                        mod_layer=i, gate_col=2 * d, name="even_out")
        else:
            o = i // 2
            n_zx = 2 * d_inner + 2 * gn
            zx = _matmul(hm, m_w_in, o, rows, n_cols=n_zx, bm=bm, bn_cap=1024, name="mamba_in")
            dtraw = _matmul(hm, m_w_in, o, rows, n_cols=heads, col0=n_zx, bm=bm, bn_cap=128, name="mamba_dt")
            xp, xs, mcp, mcs = _mamba_pre(zx, rows, d_inner, state_mamba_conv[o], m_conv_w[o], m_conv_b[o])
            mc_p.append(mcp)
            mc_s.append(mcs)
            ssd_args = (m_dt_bias[o], m_a_log[o], m_d[o], m_norm_g[o])
            yp, sp = _ssd(xp, d_inner, d_inner // gn, d_inner // gn + 1, dtraw, zx,
                          jnp.zeros((n_prompt,) + state_ssm.shape[2:], F32), *ssd_args,
                          n_seq=n_prompt, n_chunks=rows.chunks_per_seq, n_in=ROW_BLOCK, groups=groups,
                          out_dtype=BF16, name="ssd_prompt")
            z_s = _steps_to_batch_major(zx, rows, 0, d_inner, lane_block)
            dt_s = _steps_to_batch_major(dtraw, rows, 0, heads, heads)
            ys, ss = _ssd(xs, d_inner, d_inner // gn, d_inner // gn + 1, dt_s, z_s, state_ssm[o], *ssd_args,
                          n_seq=n_dec, n_chunks=1, n_in=dec_seq, groups=groups, out_dtype=F32,
                          name="ssd_decode")
            ssm_p.append(sp)
            ssm_s.append(ss)
            ymix = jnp.concatenate([yp, jnp.zeros((rows.ts, d_inner), BF16)], axis=0)
            ymix = _batch_to_steps_major(ys, ymix, rows, lane_block)
            x = _matmul(ymix, m_w_out, o, rows, n_cols=d, bm=bm_k, bn_cap=512, bk=d_inner // 2, epi="resid",
                        res=x, mods=mods, mod_layer=i, gate_col=2 * d, name="mamba_out")
        hf = _norm_mod(x, g_ffn[i], rows, mods, i, shift_col=3, scale_col=4, name="norm_ffn")
        a2 = _matmul(hf, ff_w1, i, rows, n_cols=d_ff, bm=bm, bn_cap=1024, epi="relu2", out_dtype=BF16,
                     name="ffn_up")
        x = _matmul(a2, ff_w2, i, rows, n_cols=d, bm=bm_k, bn_cap=1024, bk=d_ff // 8, epi="resid", res=x,
                    mods=mods, mod_layer=i, gate_col=5 * d, name="ffn_down")

    y = _norm_mod(x, g_final, rows, out_dtype=F32, name="norm_final")
    y_prompt = y[:rows.tp].reshape(n_prompt, seq, d)
    y_sample = jnp.swapaxes(y[rows.tp:].reshape(dec_seq, n_dec, d), 0, 1)
    return (y_prompt, y_sample, jnp.stack(sc_p), jnp.stack(sc_s), jnp.stack(v_s), jnp.stack(mc_p),
            jnp.stack(mc_s), jnp.stack(ssm_p), jnp.stack(ssm_s))
```

```python
import functools

import jax
import jax.numpy as jnp
from jax import lax
from jax.experimental import pallas as pl
from jax.experimental.pallas import tpu as pltpu

EPS = 1e-5
ROW_BLOCK = 128
SUBLANES = 8
V7X_VMEM_LIMIT = 58 * 1024 * 1024
BF16 = jnp.bfloat16
F32 = jnp.float32


def _params(n_axes, vmem_mb=None):
    limit = V7X_VMEM_LIMIT if vmem_mb is None else min(vmem_mb * 1024 * 1024, V7X_VMEM_LIMIT)
    return pltpu.CompilerParams(dimension_semantics=("arbitrary",) * n_axes, vmem_limit_bytes=limit)


def _pick_block(n, cap, *also):
    for c in (2048, 1024, 512, 256, 128):
        if c <= cap and n % c == 0 and all(a % c == 0 for a in also):
            return c
    raise ValueError(f"no lane-aligned block divides {n}")


def _dot(a, b):
    return jnp.dot(a, b, preferred_element_type=F32)


def _dot_nt(a, b):
    return lax.dot_general(a, b, (((1,), (1,)), ((), ())), preferred_element_type=F32)


def _ada_kernel(c_ref, w_ref, b_ref, o_ref, cact_ref, *, n_prompt, n_dec):
    first = jnp.logical_and(pl.program_id(0) == 0, pl.program_id(1) == 0)

    @pl.when(first)
    def _():
        c = c_ref[...]
        cact_ref[...] = (c * jax.nn.sigmoid(c)).astype(BF16)

    acc = _dot(cact_ref[...], w_ref[...].astype(BF16)) + b_ref[...]
    for b in range(n_prompt):
        row = acc[n_dec + b:n_dec + b + 1, :]
        o_ref[b * ROW_BLOCK:(b + 1) * ROW_BLOCK, :] = jnp.broadcast_to(row, (ROW_BLOCK, acc.shape[1]))
    o_ref[n_prompt * ROW_BLOCK:, :] = acc[:n_dec, :]


def _ada_all(c_prompt, c_sample, w_ada, b_ada, bn=512):
    depth, d, n6 = w_ada.shape
    n_prompt, n_dec = c_prompt.shape[0], c_sample.shape[0]
    assert n_dec == ROW_BLOCK and n6 % bn == 0
    pad = (-(n_dec + n_prompt)) % SUBLANES
    c_rows = jnp.concatenate([c_sample, c_prompt, jnp.zeros((pad, d), c_prompt.dtype)], axis=0)
    rows = c_rows.shape[0]
    n_groups = n_prompt + 1
    return pl.pallas_call(
        functools.partial(_ada_kernel, n_prompt=n_prompt, n_dec=n_dec),
        out_shape=jax.ShapeDtypeStruct((depth, n_groups * ROW_BLOCK, n6), F32),
        grid=(depth, n6 // bn),
        in_specs=[
            pl.BlockSpec((rows, d), lambda l, n: (0, 0)),
            pl.BlockSpec((None, d, bn), lambda l, n: (l, 0, n)),
            pl.BlockSpec((None, 1, bn), lambda l, n: (l, 0, n)),
        ],
        out_specs=pl.BlockSpec((None, n_groups * ROW_BLOCK, bn), lambda l, n: (l, 0, n)),
        scratch_shapes=[pltpu.VMEM((rows, d), BF16)],
        compiler_params=_params(2),
        name="ada_mod",
    )(c_rows, w_ada, b_ada.reshape(depth, 1, n6))


class _Rows:
    def __init__(self, n_prompt, seq, n_dec, dec_seq):
        self.n_prompt, self.seq, self.n_dec, self.dec_seq = n_prompt, seq, n_dec, dec_seq
        self.tp = n_prompt * seq
        self.ts = n_dec * dec_seq
        self.t = self.tp + self.ts
        assert n_dec == ROW_BLOCK and seq % ROW_BLOCK == 0
        self.chunks_per_seq = seq // ROW_BLOCK
        self.prompt_blocks = self.tp // ROW_BLOCK

    def tile_rows(self, cap):
        bm = cap
        while self.seq % bm or self.ts % bm:
            bm //= 2
        assert bm >= ROW_BLOCK
        return bm

    def group_of_tile(self, i, bm):
        return jnp.where(i * bm < self.tp, (i * bm) // self.seq, self.n_prompt)


def _norm_kernel(*refs, modulated):
    if modulated:
        x_ref, g_ref, sh_ref, sc_ref, o_ref = refs
    else:
        x_ref, g_ref, o_ref = refs
    x = x_ref[...]
    y = x * lax.rsqrt(jnp.mean(x * x, axis=-1, keepdims=True) + EPS) * g_ref[...]
    if modulated:
        bm, d = y.shape
        y3 = y.reshape(bm // ROW_BLOCK, ROW_BLOCK, d)
        y = (y3 * (1.0 + sc_ref[...])[None] + sh_ref[...][None]).reshape(bm, d)
    o_ref[...] = y.astype(o_ref.dtype)


def _norm_mod(x, g, rows, mods=None, layer=0, shift_col=0, scale_col=0, out_dtype=BF16, name="norm"):
    t, d = x.shape
    bm = rows.tile_rows(512)
    in_specs = [pl.BlockSpec((bm, d), lambda i: (i, 0)), pl.BlockSpec((1, d), lambda i: (0, 0))]
    args = [x, g.reshape(1, d)]
    if mods is not None:
        for col in (shift_col, scale_col):
            in_specs.append(pl.BlockSpec((None, ROW_BLOCK, d),
                                         lambda i, col=col: (layer, rows.group_of_tile(i, bm), col)))
            args.append(mods)
    return pl.pallas_call(
        functools.partial(_norm_kernel, modulated=mods is not None),
        out_shape=jax.ShapeDtypeStruct((t, d), out_dtype),
        grid=(t // bm,),
        in_specs=in_specs,
        out_specs=pl.BlockSpec((bm, d), lambda i: (i, 0)),
        compiler_params=_params(1),
        name=name,
    )(*args)


def _mm_kernel(*refs, nk, epi):
    if epi == "resid":
        x_ref, w_ref, res_ref, gate_ref, o_ref, *scratch = refs
    else:
        x_ref, w_ref, o_ref, *scratch = refs

    def epilogue(acc):
        if epi == "f32":
            o_ref[...] = acc
        elif epi == "relu2":
            a = jnp.maximum(acc, 0.0)
            o_ref[...] = (a * a).astype(o_ref.dtype)
        else:
            bm, bn = acc.shape
            gated = acc.reshape(bm // ROW_BLOCK, ROW_BLOCK, bn) * gate_ref[...][None]
            o_ref[...] = res_ref[...] + gated.reshape(bm, bn)

    if nk == 1:
        (wb_ref,) = scratch

        @pl.when(pl.program_id(1) == 0)
        def _():
            wb_ref[...] = w_ref[...].astype(BF16)

        epilogue(_dot(x_ref[...], wb_ref[...]))
    else:
        (acc_ref,) = scratch
        k = pl.program_id(2)

        @pl.when(k == 0)
        def _():
            acc_ref[...] = jnp.zeros_like(acc_ref)

        acc_ref[...] += _dot(x_ref[...], w_ref[...].astype(BF16))

        @pl.when(k == nk - 1)
        def _():
            epilogue(acc_ref[...])


def _matmul(x, w, layer, rows, *, n_cols, col0=0, bm, bn_cap, bk=None, epi="f32", out_dtype=F32,
            res=None, mods=None, mod_layer=0, gate_col=0, name="mm"):
    t, kdim = x.shape
    bk = kdim if bk is None else bk
    bn = _pick_block(n_cols, bn_cap, col0, gate_col)
    assert t % bm == 0 and kdim % bk == 0
    nk = kdim // bk
    in_specs = [
        pl.BlockSpec((bm, bk), lambda n, m, k: (m, k)),
        pl.BlockSpec((None, bk, bn), lambda n, m, k: (layer, k, col0 // bn + n)),
    ]
    args = [x, w]
    if epi == "resid":
        assert gate_col % bn == 0
        in_specs.append(pl.BlockSpec((bm, bn), lambda n, m, k: (m, n)))
        in_specs.append(pl.BlockSpec((None, ROW_BLOCK, bn),
                                     lambda n, m, k: (mod_layer, rows.group_of_tile(m, bm), gate_col // bn + n)))
        args += [res, mods]
    scratch = [pltpu.VMEM((bk, bn), BF16)] if nk == 1 else [pltpu.VMEM((bm, bn), F32)]
    return pl.pallas_call(
        functools.partial(_mm_kernel, nk=nk, epi=epi),
        out_shape=jax.ShapeDtypeStruct((t, n_cols), out_dtype),
        grid=(n_cols // bn, t // bm, nk),
        in_specs=in_specs,
        out_specs=pl.BlockSpec((bm, bn), lambda n, m, k: (m, n)),
        scratch_shapes=scratch,
        compiler_params=_params(3),
        name=name,
    )(*args)


def _gelu_layernorm(v, g, b):
    v = jax.nn.gelu(v)
    mu = jnp.mean(v, axis=-1, keepdims=True)
    vc = v - mu
    var = jnp.mean(vc * vc, axis=-1, keepdims=True)
    return vc * lax.rsqrt(var + EPS) * g + b


def _even_prompt_kernel(u_ref, v_ref, bg_ref, cg_ref, xi_ref, lng_ref, lnb_ref, ws_ref, bse_ref, scw_ref,
                        y_ref, tail_ref, carry_ref, *, chunks_per_seq, n_groups, gd, d_a, taps):
    j = pl.program_id(0)

    @pl.when(j % chunks_per_seq == 0)
    def _():
        carry_ref[...] = jnp.zeros_like(carry_ref)

    u = jax.nn.gelu(u_ref[...])
    vn = _gelu_layernorm(v_ref[...], lng_ref[...], lnb_ref[...])
    tri = (lax.broadcasted_iota(jnp.int32, (ROW_BLOCK, ROW_BLOCK), 0)
           >= lax.broadcasted_iota(jnp.int32, (ROW_BLOCK, ROW_BLOCK), 1))
    for g in range(n_groups):
        sl = slice(g * gd, (g + 1) * gd)
        wg = jnp.where(tri, ws_ref[g], 0.0).astype(BF16)
        s = _dot(wg, vn[:, sl].astype(BF16)) + bse_ref[:, sl]
        y_ref[:, sl] = (u[:, sl] * s).astype(y_ref.dtype)

    cx = cg_ref[...] * xi_ref[...]
    row = lax.broadcasted_iota(jnp.int32, cx.shape, 0)
    conv = cx * scw_ref[taps - 1:taps, :]
    for d in range(1, taps):
        shifted = pltpu.roll(cx, d, axis=0)
        for r in range(d):
            src = SUBLANES - d + r
            shifted = jnp.where(row == r, carry_ref[src:src + 1, :], shifted)
        conv = conv + shifted * scw_ref[taps - 1 - d:taps - d, :]
    y_ref[:, d_a:] = (bg_ref[...] * conv).astype(y_ref.dtype)
    last = cx[ROW_BLOCK - SUBLANES:, :]
    carry_ref[...] = last
    tail_ref[...] = last


def _even_sample_kernel(wsm_ref, bsm_ref, u_ref, v_ref, bg_ref, cg_ref, xi_ref, *rest,
                        n_steps, n_groups, gd, d_a, taps):
    st_refs = rest[:taps - 1]
    lng_ref, lnb_ref, scw_ref, _, y_ref, vout_ref, scn_ref, vbuf_ref, car_ref = rest[taps - 1:]
    t = pl.program_id(0)

    @pl.when(t == 0)
    def _():
        vbuf_ref[...] = jnp.zeros_like(vbuf_ref)
        for k in range(taps - 1):
            car_ref[k] = st_refs[k][...]

    u = jax.nn.gelu(u_ref[...])
    vn = _gelu_layernorm(v_ref[...], lng_ref[...], lnb_ref[...])
    vout_ref[...] = vn
    vbuf_ref[t] = vn
    for g in range(n_groups):
        sl = slice(g * gd, (g + 1) * gd)
        acc = jnp.zeros((ROW_BLOCK, gd), F32)
        for s in range(n_steps):
            coef = jnp.where(s <= t, wsm_ref[(g * n_steps + t) * n_steps + s], 0.0)
            acc = acc + coef * vbuf_ref[s, :, sl]
        sg = acc + bsm_ref[g * n_steps + t]
        y_ref[:, sl] = (u[:, sl] * sg).astype(y_ref.dtype)

    cx = cg_ref[...] * xi_ref[...]
    conv = cx * scw_ref[taps - 1:taps, :]
    for k in range(taps - 1):
        conv = conv + car_ref[k] * scw_ref[k:k + 1, :]
    y_ref[:, d_a:] = (bg_ref[...] * conv).astype(y_ref.dtype)
    scn_ref[...] = cx
    for k in range(taps - 2):
        car_ref[k] = car_ref[k + 1]
    car_ref[taps - 2] = cx


def _even_mixer(proj, rows, sc_state, ln_g, ln_b, w_s, b_s, sc_w):
    n_groups, chunk, _ = w_s.shape
    d_a = ln_g.shape[0]
    taps, d_b = sc_w.shape
    gd = d_a // n_groups
    assert chunk == ROW_BLOCK and d_a == d_b and gd % 128 == 0 and taps - 1 <= SUBLANES
    bw = d_a
    t, n_dec, n_steps = rows.t, rows.n_dec, rows.dec_seq
    assert n_steps <= chunk and n_steps >= taps - 1
    col_specs = lambda row_of: [pl.BlockSpec((ROW_BLOCK, bw), lambda j, c=c: (row_of(j), c)) for c in range(5)]
    const = lambda shape: pl.BlockSpec(shape, lambda j: (0,) * len(shape))

    bse = jnp.repeat(b_s.T, gd, axis=1)
    y, tail = pl.pallas_call(
        functools.partial(_even_prompt_kernel, chunks_per_seq=rows.chunks_per_seq, n_groups=n_groups,
                          gd=gd, d_a=d_a, taps=taps),
        out_shape=(jax.ShapeDtypeStruct((t, d_a + d_b), BF16),
                   jax.ShapeDtypeStruct((rows.n_prompt, SUBLANES, d_b), F32)),
        grid=(rows.prompt_blocks,),
        in_specs=col_specs(lambda j: j) + [
            const((1, d_a)), const((1, d_a)), const((n_groups, chunk, chunk)), const((chunk, d_a)),
            const((taps, d_b))],
        out_specs=(pl.BlockSpec((ROW_BLOCK, d_a + d_b), lambda j: (j, 0)),
                   pl.BlockSpec((None, SUBLANES, d_b), lambda j: (j // rows.chunks_per_seq, 0, 0))),
        scratch_shapes=[pltpu.VMEM((SUBLANES, d_b), F32)],
        compiler_params=_params(1),
        name="even_mixer_prompt",
    )(proj, proj, proj, proj, proj, ln_g.reshape(1, d_a), ln_b.reshape(1, d_a), w_s, bse, sc_w)

    smem = pl.BlockSpec(memory_space=pltpu.SMEM)
    state2d = sc_state.reshape(n_dec, (taps - 1) * d_b)
    first_new = n_steps - (taps - 1)
    y, v_new, sc_new = pl.pallas_call(
        functools.partial(_even_sample_kernel, n_steps=n_steps, n_groups=n_groups, gd=gd, d_a=d_a, taps=taps),
        out_shape=(jax.ShapeDtypeStruct((t, d_a + d_b), BF16),
                   jax.ShapeDtypeStruct((n_dec, n_steps * d_a), F32),
                   jax.ShapeDtypeStruct((n_dec, (taps - 1) * d_b), F32)),
        grid=(n_steps,),
        in_specs=[smem, smem] + col_specs(lambda j: rows.prompt_blocks + j)
        + [pl.BlockSpec((n_dec, d_b), lambda j, k=k: (0, k)) for k in range(taps - 1)]
        + [const((1, d_a)), const((1, d_a)), const((taps, d_b)), pl.BlockSpec(memory_space=pl.ANY)],
        out_specs=(pl.BlockSpec((ROW_BLOCK, d_a + d_b), lambda j: (rows.prompt_blocks + j, 0)),
                   pl.BlockSpec((n_dec, d_a), lambda j: (0, j)),
                   pl.BlockSpec((n_dec, d_b), lambda j: (0, jnp.maximum(j - first_new, 0)))),
        scratch_shapes=[pltpu.VMEM((n_steps, n_dec, d_a), F32), pltpu.VMEM((taps - 1, n_dec, d_b), F32)],
        input_output_aliases={9 + taps: 0},
        compiler_params=_params(1),
        name="even_mixer_decode",
    )(w_s[:, :n_steps, :n_steps].reshape(-1), b_s[:, :n_steps].reshape(-1), proj, proj, proj, proj, proj,
      *([state2d] * (taps - 1)), ln_g.reshape(1, d_a), ln_b.reshape(1, d_a), sc_w, y)
    return (y, tail[:, SUBLANES - (taps - 1):, :], sc_new.reshape(n_dec, taps - 1, d_b),
            v_new.reshape(n_dec, n_steps, d_a))


def _silu(x):
    return x * jax.nn.sigmoid(x)


def _mpre_prompt_kernel(x_ref, w_ref, b_ref, o_ref, tail_ref, carry_ref, *, chunks_per_seq, taps):
    j = pl.program_id(1)

    @pl.when(j % chunks_per_seq == 0)
    def _():
        carry_ref[...] = jnp.zeros_like(carry_ref)

    x = x_ref[...]
    row = lax.broadcasted_iota(jnp.int32, x.shape, 0)
    acc = x * w_ref[taps - 1:taps, :]
    for d in range(1, taps):
        shifted = pltpu.roll(x, d, axis=0)
        for r in range(d):
            src = SUBLANES - d + r
            shifted = jnp.where(row == r, carry_ref[src:src + 1, :], shifted)
        acc = acc + shifted * w_ref[taps - 1 - d:taps - d, :]
    o_ref[...] = _silu(acc + b_ref[...])
    last = x[ROW_BLOCK - SUBLANES:, :]
    carry_ref[...] = last
    tail_ref[...] = last


def _mpre_sample_kernel(x_ref, *rest, taps):
    st_refs = rest[:taps - 1]
    w_ref, b_ref, o_ref, new_ref, car_ref = rest[taps - 1:]
    t = pl.program_id(1)

    @pl.when(t == 0)
    def _():
        for k in range(taps - 1):
            car_ref[k] = st_refs[k][...]

    x = x_ref[...]
    acc = x * w_ref[taps - 1:taps, :]
    for k in range(taps - 1):
        acc = acc + car_ref[k] * w_ref[k:k + 1, :]
    o_ref[...] = _silu(acc + b_ref[...])
    new_ref[...] = x
    for k in range(taps - 2):
        car_ref[k] = car_ref[k + 1]
    car_ref[taps - 2] = x


def _mamba_pre(zx, rows, xbc_col0, conv_state, conv_w, conv_b):
    taps, conv_dim = conv_w.shape
    bw = _pick_block(conv_dim, 2048, xbc_col0)
    assert taps - 1 <= SUBLANES
    ncb = conv_dim // bw
    cb0 = xbc_col0 // bw
    n_dec, n_steps = rows.n_dec, rows.dec_seq
    wspec = pl.BlockSpec((taps, bw), lambda c, j: (0, c))
    bspec = pl.BlockSpec((1, bw), lambda c, j: (0, c))
    xp, tail = pl.pallas_call(
        functools.partial(_mpre_prompt_kernel, chunks_per_seq=rows.chunks_per_seq, taps=taps),
        out_shape=(jax.ShapeDtypeStruct((rows.tp, conv_dim), F32),
                   jax.ShapeDtypeStruct((rows.n_prompt, SUBLANES, conv_dim), F32)),
        grid=(ncb, rows.prompt_blocks),
        in_specs=[pl.BlockSpec((ROW_BLOCK, bw), lambda c, j: (j, cb0 + c)), wspec, bspec],
        out_specs=(pl.BlockSpec((ROW_BLOCK, bw), lambda c, j: (j, c)),
                   pl.BlockSpec((None, SUBLANES, bw), lambda c, j: (j // rows.chunks_per_seq, 0, c))),
        scratch_shapes=[pltpu.VMEM((SUBLANES, bw), F32)],
        compiler_params=_params(2),
        name="mamba_conv_prompt",
    )(zx, conv_w, conv_b.reshape(1, conv_dim))

    state2d = conv_state.reshape(n_dec, (taps - 1) * conv_dim)
    first_new = n_steps - (taps - 1)
    xs, new = pl.pallas_call(
        functools.partial(_mpre_sample_kernel, taps=taps),
        out_shape=(jax.ShapeDtypeStruct((n_dec, n_steps * conv_dim), F32),
                   jax.ShapeDtypeStruct((n_dec, (taps - 1) * conv_dim), F32)),
        grid=(ncb, n_steps),
        in_specs=[pl.BlockSpec((ROW_BLOCK, bw), lambda c, j: (rows.prompt_blocks + j, cb0 + c))]
        + [pl.BlockSpec((n_dec, bw), lambda c, j, k=k: (0, k * ncb + c)) for k in range(taps - 1)]
        + [wspec, bspec],
        out_specs=(pl.BlockSpec((n_dec, bw), lambda c, j: (0, j * ncb + c)),
                   pl.BlockSpec((n_dec, bw), lambda c, j: (0, jnp.maximum(j - first_new, 0) * ncb + c))),
        scratch_shapes=[pltpu.VMEM((taps - 1, n_dec, bw), F32)],
        compiler_params=_params(2),
        name="mamba_conv_decode",
    )(zx, *([state2d] * (taps - 1)), conv_w, conv_b.reshape(1, conv_dim))
    return (xp, xs.reshape(n_dec * n_steps, conv_dim), tail[:, SUBLANES - (taps - 1):, :],
            new.reshape(n_dec, taps - 1, conv_dim))


def _copy_kernel(x_ref, o_ref):
    o_ref[...] = x_ref[...].astype(o_ref.dtype)


def _steps_to_batch_major(x, rows, col0, n_cols, bw):
    assert n_cols % bw == 0 and col0 % bw == 0
    nc = n_cols // bw
    out = pl.pallas_call(
        _copy_kernel,
        out_shape=jax.ShapeDtypeStruct((rows.n_dec, rows.dec_seq * n_cols), F32),
        grid=(rows.dec_seq, nc),
        in_specs=[pl.BlockSpec((ROW_BLOCK, bw), lambda s, c: (rows.prompt_blocks + s, col0 // bw + c))],
        out_specs=pl.BlockSpec((rows.n_dec, bw), lambda s, c: (0, s * nc + c)),
        compiler_params=_params(2),
        name="to_batch_major",
    )(x)
    return out.reshape(rows.n_dec * rows.dec_seq, n_cols)


def _batch_to_steps_major(y_bt, into, rows, bw):
    c = y_bt.shape[1]
    assert c % bw == 0
    nc = c // bw
    return pl.pallas_call(
        lambda x_ref, _, o_ref: _copy_kernel(x_ref, o_ref),
        out_shape=jax.ShapeDtypeStruct(into.shape, into.dtype),
        grid=(rows.dec_seq, nc),
        in_specs=[pl.BlockSpec((rows.n_dec, bw), lambda s, j: (0, s * nc + j)),
                  pl.BlockSpec(memory_space=pl.ANY)],
        out_specs=pl.BlockSpec((ROW_BLOCK, bw), lambda s, j: (rows.prompt_blocks + s, j)),
        input_output_aliases={1: 0},
        compiler_params=_params(2),
        name="to_steps_major",
    )(y_bt.reshape(rows.n_dec, rows.dec_seq * c), into)


LOG2E = 1.4426950408889634
LANES = 128


def _cumsum_rows(a):
    row = lax.broadcasted_iota(jnp.int32, a.shape, 0)
    shift = 1
    while shift < a.shape[0]:
        a = a + jnp.where(row >= shift, pltpu.roll(a, shift, axis=0), 0.0)
        shift *= 2
    return a


def _gate_norm(y, x, z, dsk, ng):
    yg = (y + dsk * x) * _silu(z)
    return yg * lax.rsqrt(jnp.mean(yg * yg, axis=-1, keepdims=True) + EPS) * ng


def _ssd_prompt_kernel(x_ref, b_ref, c_ref, dt_ref, z_ref, dtb_ref, alog_ref, dsk_ref, ng_ref, *rest,
                       heads, hd, ns, groups):
    y_ref, s_ref, xst_ref, yb_ref = rest[-4:]
    L = ROW_BLOCK
    hpg = heads // groups
    gw = hpg * hd
    hpt = LANES // hd

    @pl.when(pl.program_id(1) == 0)
    def _():
        s_ref[...] = jnp.zeros_like(s_ref)

    dt = jax.nn.softplus(dt_ref[...] + dtb_ref[...])
    cum2 = _cumsum_rows(dt * (-jnp.exp(alog_ref[...]) * LOG2E))
    cum2_t = cum2.T
    dt_t = dt.T
    crow = cum2_t - jnp.log2(dt_t)
    ri = lax.broadcasted_iota(jnp.int32, (L, L), 0)
    li = lax.broadcasted_iota(jnp.int32, (L, L), 1)
    tri = ri >= li
    lane_head = li // hd

    for g in range(groups):
        bg = b_ref[:, g * ns:(g + 1) * ns].astype(BF16)
        cg = c_ref[:, g * ns:(g + 1) * ns].astype(BF16)
        cb = _dot_nt(cg, bg)
        s_g = s_ref[g * hpg:(g + 1) * hpg].reshape(gw, ns)
        cs = _dot_nt(cg, s_g.astype(BF16))
        for j in range(gw // LANES):
            lanes = slice(g * gw + j * LANES, g * gw + (j + 1) * LANES)
            xp = x_ref[:, lanes]
            ws, blocks, cum_e = [], [], None
            for k in range(hpt):
                h = g * hpg + j * hpt + k
                ccol = jnp.broadcast_to(cum2[:, h:h + 1], (L, L))
                ws.append(jnp.where(tri, cb * jnp.exp2(ccol - crow[h:h + 1, :]), 0.0).astype(BF16))
                blocks.append(jnp.where(lane_head == k, xp, 0.0).astype(BF16))
                cum_e = ccol if k == 0 else jnp.where(lane_head >= k, ccol, cum_e)
            intra = _dot(jnp.concatenate(ws, axis=1), jnp.concatenate(blocks, axis=0))
            yb_ref[:, j * LANES:(j + 1) * LANES] = intra + cs[:, j * LANES:(j + 1) * LANES] * jnp.exp2(cum_e)

        x_t = x_ref[:, g * gw:(g + 1) * gw].T
        for r in range(hpg):
            h = g * hpg + r
            tail = jnp.exp2(cum2_t[h:h + 1, L - 1:L] - cum2_t[h:h + 1, :]) * dt_t[h:h + 1, :]
            xst_ref[r * hd:(r + 1) * hd, :] = (x_t[r * hd:(r + 1) * hd, :] * tail).astype(BF16)
        upd = _dot(xst_ref[...], bg)
        for r in range(hpg):
            h = g * hpg + r
            dec = jnp.exp2(jnp.broadcast_to(cum2[L - 1:L, h:h + 1], (hd, ns)))
            s_ref[h] = s_g[r * hd:(r + 1) * hd, :] * dec + upd[r * hd:(r + 1) * hd, :]

        sl = slice(g * gw, (g + 1) * gw)
        y_ref[:, sl] = _gate_norm(yb_ref[...], x_ref[:, sl], z_ref[:, sl], dsk_ref[:, sl],
                                  ng_ref[:, sl]).astype(y_ref.dtype)


def _ssd_decode_kernel(x_ref, b_ref, c_ref, dt_ref, z_ref, s0_ref, dtb_ref, alog_ref, dsk_ref, ng_ref, *rest,
                       n_steps, heads, hd, ns, groups):
    y_ref, s_ref = rest[-2:]
    T = n_steps
    hpg = heads // groups
    gw = hpg * hd
    hpt = LANES // hd
    d_inner = heads * hd

    dt = jax.nn.softplus(dt_ref[...] + dtb_ref[...])
    cum2 = _cumsum_rows(dt * (-jnp.exp(alog_ref[...]) * LOG2E))
    lane_head = lax.broadcasted_iota(jnp.int32, (T, LANES), 1) // hd
    cum_tiles, dt_tiles = [], []
    for j in range(d_inner // LANES):
        ce = de = None
        for k in range(hpt):
            h = j * hpt + k
            cc = jnp.broadcast_to(cum2[:, h:h + 1], (T, LANES))
            dd = jnp.broadcast_to(dt[:, h:h + 1], (T, LANES))
            ce = cc if k == 0 else jnp.where(lane_head >= k, cc, ce)
            de = dd if k == 0 else jnp.where(lane_head >= k, dd, de)
        cum_tiles.append(ce)
        dt_tiles.append(de)
    cum_e = jnp.concatenate(cum_tiles, axis=1)
    dt_e = jnp.concatenate(dt_tiles, axis=1)

    x = x_ref[...]
    xdt = x * dt_e
    xs = x * (jnp.exp2(cum_e[T - 1:T, :] - cum_e) * dt_e)
    step = lax.broadcasted_iota(jnp.int32, (T, 1), 0)
    acc = jnp.zeros((T, d_inner), F32)
    for s in range(T):
        e = jnp.exp2(jnp.minimum(cum_e - cum_e[s:s + 1, :], 0.0)) * xdt[s:s + 1, :]
        parts = []
        for g in range(groups):
            cb = jnp.sum(c_ref[:, g * ns:(g + 1) * ns] * b_ref[s:s + 1, g * ns:(g + 1) * ns], axis=-1, keepdims=True)
            parts.append(e[:, g * gw:(g + 1) * gw] * jnp.where(step >= s, cb, 0.0))
        acc = acc + jnp.concatenate(parts, axis=1)

    ecum = jnp.exp2(cum_e)
    pad = jnp.zeros((2 * SUBLANES - T, ns), F32)
    for g in range(groups):
        sl = slice(g * gw, (g + 1) * gw)
        s_g = s0_ref[g * hpg:(g + 1) * hpg].reshape(gw, ns)
        c16 = jnp.concatenate([c_ref[:, g * ns:(g + 1) * ns], pad], axis=0).astype(BF16)
        cs = _dot_nt(c16, s_g.astype(BF16))[:T]
        upd = lax.dot_general(xs[:, sl], b_ref[:, g * ns:(g + 1) * ns], (((0,), (0,)), ((), ())),
                              preferred_element_type=F32)
        for r in range(hpg):
            h = g * hpg + r
            dec = jnp.exp2(jnp.broadcast_to(cum2[T - 1:T, h:h + 1], (hd, ns)))
            s_ref[h] = s_g[r * hd:(r + 1) * hd, :] * dec + upd[r * hd:(r + 1) * hd, :]
        y_ref[:, sl] = _gate_norm(acc[:, sl] + cs * ecum[:, sl], x[:, sl], z_ref[:, sl], dsk_ref[:, sl],
                                  ng_ref[:, sl])


def _ssd_consts(dt_bias, a_log, d_skip, norm_g, hd):
    heads, d_inner = a_log.shape[0], norm_g.shape[0]
    return (dt_bias.reshape(1, heads), a_log.reshape(1, heads), jnp.repeat(d_skip, hd).reshape(1, d_inner),
            norm_g.reshape(1, d_inner))


def _ssd_prompt(xbc, dtraw, zx, consts, rows, layer, n_layers, prev_state, *, heads, hd, ns, groups):
    d_inner, gn = heads * hd, groups * ns
    cps = rows.chunks_per_seq
    row = lambda b, c: b * cps + c
    const = lambda width: pl.BlockSpec((1, width), lambda b, c: (0, 0))
    extra = [] if prev_state is None else [prev_state]
    n_in = 9 + len(extra)
    return pl.pallas_call(
        functools.partial(_ssd_prompt_kernel, heads=heads, hd=hd, ns=ns, groups=groups),
        out_shape=(jax.ShapeDtypeStruct((rows.t, d_inner), BF16),
                   jax.ShapeDtypeStruct((n_layers, rows.n_prompt, heads, hd, ns), F32)),
        grid=(rows.n_prompt, cps),
        in_specs=[
            pl.BlockSpec((ROW_BLOCK, d_inner), lambda b, c: (row(b, c), 0)),
            pl.BlockSpec((ROW_BLOCK, gn), lambda b, c: (row(b, c), d_inner // gn)),
            pl.BlockSpec((ROW_BLOCK, gn), lambda b, c: (row(b, c), d_inner // gn + 1)),
            pl.BlockSpec((ROW_BLOCK, heads), lambda b, c: (row(b, c), 0)),
            pl.BlockSpec((ROW_BLOCK, d_inner), lambda b, c: (row(b, c), 0)),
            const(heads), const(heads), const(d_inner), const(d_inner),
        ] + [pl.BlockSpec(memory_space=pl.ANY)] * len(extra),
        out_specs=(pl.BlockSpec((ROW_BLOCK, d_inner), lambda b, c: (row(b, c), 0)),
                   pl.BlockSpec((None, None, heads, hd, ns), lambda b, c: (layer, b, 0, 0, 0))),
        scratch_shapes=[pltpu.VMEM((d_inner // groups, ROW_BLOCK), BF16),
                        pltpu.VMEM((ROW_BLOCK, d_inner // groups), F32)],
        input_output_aliases={} if prev_state is None else {n_in - 1: 1},
        compiler_params=_params(2),
        name="ssd_prompt",
    )(xbc, xbc, xbc, dtraw, zx, *consts, *extra)


def _ssd_decode(xbc, dtraw, z, state, consts, rows, layer, prev_state, *, heads, hd, ns, groups):
    d_inner, gn = heads * hd, groups * ns
    n = rows.dec_seq
    const = lambda width: pl.BlockSpec((1, width), lambda b: (0, 0))
    extra = [] if prev_state is None else [prev_state]
    n_in = 10 + len(extra)
    state_spec = pl.BlockSpec((None, None, heads, hd, ns), lambda b: (layer, b, 0, 0, 0))
    return pl.pallas_call(
        functools.partial(_ssd_decode_kernel, n_steps=n, heads=heads, hd=hd, ns=ns, groups=groups),
        out_shape=(jax.ShapeDtypeStruct((rows.ts, d_inner), F32),
                   jax.ShapeDtypeStruct(state.shape, F32)),
        grid=(rows.n_dec,),
        in_specs=[
            pl.BlockSpec((n, d_inner), lambda b: (b, 0)),
            pl.BlockSpec((n, gn), lambda b: (b, d_inner // gn)),
            pl.BlockSpec((n, gn), lambda b: (b, d_inner // gn + 1)),
            pl.BlockSpec((n, heads), lambda b: (b, 0)),
            pl.BlockSpec((n, d_inner), lambda b: (b, 0)),
            state_spec,
            const(heads), const(heads), const(d_inner), const(d_inner),
        ] + [pl.BlockSpec(memory_space=pl.ANY)] * len(extra),
        out_specs=(pl.BlockSpec((n, d_inner), lambda b: (b, 0)), state_spec),
        input_output_aliases={} if prev_state is None else {n_in - 1: 1},
        compiler_params=_params(1),
        name="ssd_decode",
    )(xbc, xbc, xbc, dtraw, z, state, *consts, *extra)


def kernel(x_prompt, x_sample, state_shortconv, state_mamba_conv, state_ssm, c_prompt, c_sample, w_ada, b_ada, g_mix, g_ffn, ev_w_in, ev_ln_g, ev_ln_b, ev_w_s, ev_b_s, ev_sc_w, ev_w_out, m_w_in, m_conv_w, m_conv_b, m_dt_bias, m_a_log, m_d, m_norm_g, m_w_out, ff_w1, ff_w2, g_final):
    n_prompt, seq, d = x_prompt.shape
    n_dec, dec_seq, _ = x_sample.shape
    rows = _Rows(n_prompt, seq, n_dec, dec_seq)
    depth = w_ada.shape[0]
    d_ff = ff_w1.shape[2]
    d_a = ev_ln_g.shape[1]
    d_b = ev_sc_w.shape[2]
    heads = m_a_log.shape[1]
    d_inner = m_norm_g.shape[1]
    conv_dim = m_conv_w.shape[2]
    ns = state_ssm.shape[-1]
    groups = (conv_dim - d_inner) // (2 * ns)
    gn = groups * ns
    assert d_inner % gn == 0 and m_w_in.shape[2] == 2 * d_inner + 2 * gn + heads

    bm = rows.tile_rows(512)
    bm_k = rows.tile_rows(1024)
    lane_block = _pick_block(d_inner, 2048)

    x = jnp.concatenate([x_prompt.reshape(rows.tp, d),
                         jnp.swapaxes(x_sample, 0, 1).reshape(rows.ts, d)], axis=0)
    mods = _ada_all(c_prompt, c_sample, w_ada, b_ada)

    sc_p, sc_s, v_s, mc_p, mc_s = [], [], [], [], []
    ssm_p = ssm_s = None
    n_odd = state_ssm.shape[0]
    hd = d_inner // heads
    for i in range(depth):
        hm = _norm_mod(x, g_mix[i], rows, mods, i, shift_col=0, scale_col=1, name="norm_mix")
        if i % 2 == 0:
            e = i // 2
            n_proj = ev_w_in.shape[2]
            proj = _matmul(hm, ev_w_in, e, rows, n_cols=n_proj, bm=bm, bn_cap=1024, name="even_in")
            ymix, scp, scs, vs = _even_mixer(proj, rows, state_shortconv[e], ev_ln_g[e], ev_ln_b[e],
                                             ev_w_s[e], ev_b_s[e], ev_sc_w[e])
            sc_p.append(scp)
            sc_s.append(scs)
            v_s.append(vs)
            x = _matmul(ymix, ev_w_out, e, rows, n_cols=d, bm=bm, bn_cap=512, epi="resid", res=x, mods=mods,
                        mod_layer=i, gate_col=2 * d, name="even_out")
        else:
            o = i // 2
            n_zx = 2 * d_inner + 2 * gn
            zx = _matmul(hm, m_w_in, o, rows, n_cols=n_zx, bm=bm, bn_cap=1024, name="mamba_in")
            dtraw = _matmul(hm, m_w_in, o, rows, n_cols=heads, col0=n_zx, bm=bm, bn_cap=128, name="mamba_dt")
            xp, xs, mcp, mcs = _mamba_pre(zx, rows, d_inner, state_mamba_conv[o], m_conv_w[o], m_conv_b[o])
            mc_p.append(mcp)
            mc_s.append(mcs)
            consts = _ssd_consts(m_dt_bias[o], m_a_log[o], m_d[o], m_norm_g[o], hd)
            dims = dict(heads=heads, hd=hd, ns=ns, groups=groups)
            ymix, ssm_p = _ssd_prompt(xp, dtraw, zx, consts, rows, o, n_odd, ssm_p, **dims)
            z_s = _steps_to_batch_major(zx, rows, 0, d_inner, lane_block)
            dt_s = _steps_to_batch_major(dtraw, rows, 0, heads, heads)
            ys, ssm_s = _ssd_decode(xs, dt_s, z_s, state_ssm, consts, rows, o, ssm_s, **dims)
            ymix = _batch_to_steps_major(ys, ymix, rows, lane_block)
            x = _matmul(ymix, m_w_out, o, rows, n_cols=d, bm=bm_k, bn_cap=512, bk=d_inner // 2, epi="resid",
                        res=x, mods=mods, mod_layer=i, gate_col=2 * d, name="mamba_out")
        hf = _norm_mod(x, g_ffn[i], rows, mods, i, shift_col=3, scale_col=4, name="norm_ffn")
        a2 = _matmul(hf, ff_w1, i, rows, n_cols=d_ff, bm=bm, bn_cap=1024, epi="relu2", out_dtype=BF16,
                     name="ffn_up")
        x = _matmul(a2, ff_w2, i, rows, n_cols=d, bm=bm_k, bn_cap=1024, bk=d_ff // 8, epi="resid", res=x,
                    mods=mods, mod_layer=i, gate_col=5 * d, name="ffn_down")

    y = _norm_mod(x, g_final, rows, out_dtype=F32, name="norm_final")
    y_prompt = y[:rows.tp].reshape(n_prompt, seq, d)
    y_sample = jnp.swapaxes(y[rows.tp:].reshape(dec_seq, n_dec, d), 0, 1)
    return (y_prompt, y_sample, jnp.stack(sc_p), jnp.stack(sc_s), jnp.stack(v_s), jnp.stack(mc_p),
            jnp.stack(mc_s), ssm_p, ssm_s)
```

```python
import functools

import jax
import jax.numpy as jnp
from jax import lax
from jax.experimental import pallas as pl
from jax.experimental.pallas import tpu as pltpu

EPS = 1e-5
ROW_BLOCK = 128
SUBLANES = 8
LANES = 128
V7X_VMEM_LIMIT = 58 * 1024 * 1024
BF16 = jnp.bfloat16
F32 = jnp.float32


def _params(n_axes, vmem_mb=None):
    limit = V7X_VMEM_LIMIT if vmem_mb is None else min(vmem_mb * 1024 * 1024, V7X_VMEM_LIMIT)
    return pltpu.CompilerParams(dimension_semantics=("arbitrary",) * n_axes, vmem_limit_bytes=limit)


def _pick_block(n, cap, *also):
    for c in (2048, 1024, 512, 256, 128):
        if c <= cap and n % c == 0 and all(a % c == 0 for a in also):
            return c
    raise ValueError(f"no lane-aligned block divides {n}")


def _dot(a, b):
    return jnp.dot(a, b, preferred_element_type=F32)


def _dot_nt(a, b):
    return lax.dot_general(a, b, (((1,), (1,)), ((), ())), preferred_element_type=F32)


def _ada_kernel(c_ref, w_ref, b_ref, o_ref, cact_ref, *, n_prompt, n_dec):
    first = jnp.logical_and(pl.program_id(0) == 0, pl.program_id(1) == 0)

    @pl.when(first)
    def _():
        c = c_ref[...]
        cact_ref[...] = (c * jax.nn.sigmoid(c)).astype(BF16)

    acc = _dot(cact_ref[...], w_ref[...].astype(BF16)) + b_ref[...]
    for b in range(n_prompt):
        row = acc[n_dec + b:n_dec + b + 1, :]
        o_ref[b * ROW_BLOCK:(b + 1) * ROW_BLOCK, :] = jnp.broadcast_to(row, (ROW_BLOCK, acc.shape[1]))
    o_ref[n_prompt * ROW_BLOCK:, :] = acc[:n_dec, :]


def _ada_all(c_prompt, c_sample, w_ada, b_ada, bn=1024):
    depth, d, n6 = w_ada.shape
    n_prompt, n_dec = c_prompt.shape[0], c_sample.shape[0]
    assert n_dec == ROW_BLOCK and n6 % bn == 0
    pad = (-(n_dec + n_prompt)) % SUBLANES
    c_rows = jnp.concatenate([c_sample, c_prompt, jnp.zeros((pad, d), c_prompt.dtype)], axis=0)
    rows = c_rows.shape[0]
    n_groups = n_prompt + 1
    return pl.pallas_call(
        functools.partial(_ada_kernel, n_prompt=n_prompt, n_dec=n_dec),
        out_shape=jax.ShapeDtypeStruct((depth, n_groups * ROW_BLOCK, n6), F32),
        grid=(depth, n6 // bn),
        in_specs=[
            pl.BlockSpec((rows, d), lambda l, n: (0, 0)),
            pl.BlockSpec((None, d, bn), lambda l, n: (l, 0, n)),
            pl.BlockSpec((None, 1, bn), lambda l, n: (l, 0, n)),
        ],
        out_specs=pl.BlockSpec((None, n_groups * ROW_BLOCK, bn), lambda l, n: (l, 0, n)),
        scratch_shapes=[pltpu.VMEM((rows, d), BF16)],
        compiler_params=_params(2),
        name="ada_mod",
    )(c_rows, w_ada, b_ada.reshape(depth, 1, n6))


class _Rows:
    def __init__(self, n_prompt, seq, n_dec, dec_seq):
        self.n_prompt, self.seq, self.n_dec, self.dec_seq = n_prompt, seq, n_dec, dec_seq
        self.tp = n_prompt * seq
        self.ts = n_dec * dec_seq
        self.t = self.tp + self.ts
        assert n_dec == ROW_BLOCK and seq % ROW_BLOCK == 0
        self.chunks_per_seq = seq // ROW_BLOCK
        self.prompt_blocks = self.tp // ROW_BLOCK

    def tile_rows(self, cap):
        bm = cap
        while self.seq % bm or self.ts % bm:
            bm //= 2
        assert bm >= ROW_BLOCK
        return bm

    def group_of_tile(self, i, bm):
        return jnp.where(i * bm < self.tp, (i * bm) // self.seq, self.n_prompt)


def _norm_kernel(*refs, modulated):
    if modulated:
        x_ref, g_ref, sh_ref, sc_ref, o_ref = refs
    else:
        x_ref, g_ref, o_ref = refs
    x = x_ref[...]
    y = x * lax.rsqrt(jnp.mean(x * x, axis=-1, keepdims=True) + EPS) * g_ref[...]
    if modulated:
        bm, d = y.shape
        y3 = y.reshape(bm // ROW_BLOCK, ROW_BLOCK, d)
        y = (y3 * (1.0 + sc_ref[...])[None] + sh_ref[...][None]).reshape(bm, d)
    o_ref[...] = y.astype(o_ref.dtype)


def _norm_mod(x, g, rows, mods=None, layer=0, shift_col=0, scale_col=0, out_dtype=BF16, name="norm"):
    t, d = x.shape
    bm = rows.tile_rows(512)
    in_specs = [pl.BlockSpec((bm, d), lambda i: (i, 0)), pl.BlockSpec((1, d), lambda i: (0, 0))]
    args = [x, g.reshape(1, d)]
    if mods is not None:
        for col in (shift_col, scale_col):
            in_specs.append(pl.BlockSpec((None, ROW_BLOCK, d),
                                         lambda i, col=col: (layer, rows.group_of_tile(i, bm), col)))
            args.append(mods)
    return pl.pallas_call(
        functools.partial(_norm_kernel, modulated=mods is not None),
        out_shape=jax.ShapeDtypeStruct((t, d), out_dtype),
        grid=(t // bm,),
        in_specs=in_specs,
        out_specs=pl.BlockSpec((bm, d), lambda i: (i, 0)),
        compiler_params=_params(1),
        name=name,
    )(*args)


def _norm_final(x, g, rows):
    _, d = x.shape
    bm = rows.tile_rows(512)
    body = functools.partial(_norm_kernel, modulated=False)
    g_spec = pl.BlockSpec((1, d), lambda i: (0, 0))
    y_prompt = pl.pallas_call(
        body,
        out_shape=jax.ShapeDtypeStruct((rows.tp, d), F32),
        grid=(rows.tp // bm,),
        in_specs=[pl.BlockSpec((bm, d), lambda i: (i, 0)), g_spec],
        out_specs=pl.BlockSpec((bm, d), lambda i: (i, 0)),
        compiler_params=_params(1),
        name="norm_final_prompt",
    )(x, g.reshape(1, d))
    y_decode = pl.pallas_call(
        body,
        out_shape=jax.ShapeDtypeStruct((rows.n_dec, rows.dec_seq * d), F32),
        grid=(rows.dec_seq,),
        in_specs=[pl.BlockSpec((ROW_BLOCK, d), lambda s: (rows.prompt_blocks + s, 0)), g_spec],
        out_specs=pl.BlockSpec((rows.n_dec, d), lambda s: (0, s)),
        compiler_params=_params(1),
        name="norm_final_decode",
    )(x, g.reshape(1, d))
    return (y_prompt.reshape(rows.n_prompt, rows.seq, d), y_decode.reshape(rows.n_dec, rows.dec_seq, d))


def _mm_epilogue(acc, epi, o_ref, res_ref, gate_ref):
    if epi == "f32":
        o_ref[...] = acc
    elif epi == "relu2":
        a = jnp.maximum(acc, 0.0)
        o_ref[...] = (a * a).astype(o_ref.dtype)
    else:
        bm, bn = acc.shape
        gated = acc.reshape(bm // ROW_BLOCK, ROW_BLOCK, bn) * gate_ref[...][None]
        o_ref[...] = res_ref[...] + gated.reshape(bm, bn)


def _mm_stationary_kernel(*refs, layer, col0, ck, n_chunks, per_step, epi):
    if epi == "resid":
        x_ref, w_hbm, res_ref, gate_ref, o_ref, wb_ref, stage_ref, sem = refs
    else:
        x_ref, w_hbm, o_ref, wb_ref, stage_ref, sem = refs
        res_ref = gate_ref = None
    n, m = pl.program_id(0), pl.program_id(1)
    bn = wb_ref.shape[2]
    slot = n % 2

    def chunk_copy(tile, chunk, buf):
        src = w_hbm.at[layer, pl.ds(pl.multiple_of(chunk * ck, ck), ck),
                       pl.ds(pl.multiple_of(col0 + tile * bn, LANES), bn)]
        return pltpu.make_async_copy(src, stage_ref.at[buf], sem.at[buf])

    @pl.when(jnp.logical_and(n == 0, m == 0))
    def _():
        chunk_copy(0, 0, 0).start()
        for c in range(n_chunks):
            if c + 1 < n_chunks:
                chunk_copy(0, c + 1, (c + 1) % 2).start()
            chunk_copy(0, c, c % 2).wait()
            wb_ref[0, c * ck:(c + 1) * ck, :] = stage_ref[c % 2].astype(BF16)

    def prefetch(action):
        for i in range(per_step):
            chunk = m * per_step + i

            @pl.when(jnp.logical_and(n + 1 < pl.num_programs(0), chunk < n_chunks))
            def _():
                action(chunk, i)

    prefetch(lambda chunk, buf: chunk_copy(n + 1, chunk, buf).start())
    _mm_epilogue(_dot(x_ref[...], wb_ref[slot]), epi, o_ref, res_ref, gate_ref)

    def land(chunk, buf):
        chunk_copy(n + 1, chunk, buf).wait()
        wb_ref[1 - slot, pl.ds(pl.multiple_of(chunk * ck, ck), ck), :] = stage_ref[buf].astype(BF16)

    prefetch(land)


def _matmul_stationary(x, w, layer, rows, *, n_cols, col0=0, bm, bn_cap, ck=512, epi="f32", out_dtype=F32,
                       res=None, mods=None, mod_layer=0, gate_col=0, name="mm"):
    t, kdim = x.shape
    bn = _pick_block(n_cols, bn_cap, col0, gate_col)
    ck = min(ck, kdim)
    assert t % bm == 0 and kdim % ck == 0
    n_chunks = kdim // ck
    n_row_tiles = t // bm
    per_step = -(-n_chunks // n_row_tiles)
    in_specs = [pl.BlockSpec((bm, kdim), lambda n, m: (m, 0)), pl.BlockSpec(memory_space=pl.ANY)]
    args = [x, w]
    if epi == "resid":
        in_specs.append(pl.BlockSpec((bm, bn), lambda n, m: (m, n)))
        in_specs.append(pl.BlockSpec((None, ROW_BLOCK, bn),
                                     lambda n, m: (mod_layer, rows.group_of_tile(m, bm), gate_col // bn + n)))
        args += [res, mods]
    return pl.pallas_call(
        functools.partial(_mm_stationary_kernel, layer=layer, col0=col0, ck=ck, n_chunks=n_chunks,
                          per_step=per_step, epi=epi),
        out_shape=jax.ShapeDtypeStruct((t, n_cols), out_dtype),
        grid=(n_cols // bn, n_row_tiles),
        in_specs=in_specs,
        out_specs=pl.BlockSpec((bm, bn), lambda n, m: (m, n)),
        scratch_shapes=[pltpu.VMEM((2, kdim, bn), BF16), pltpu.VMEM((max(2, per_step), ck, bn), F32),
                        pltpu.SemaphoreType.DMA((max(2, per_step),))],
        compiler_params=_params(2),
        name=name,
    )(*args)


def _mm_kernel(*refs, nk, epi):
    if epi == "resid":
        x_ref, w_ref, res_ref, gate_ref, o_ref, *scratch = refs
    else:
        x_ref, w_ref, o_ref, *scratch = refs
        res_ref = gate_ref = None

    def epilogue(acc):
        _mm_epilogue(acc, epi, o_ref, res_ref, gate_ref)

    if nk == 1:
        (wb_ref,) = scratch

        @pl.when(pl.program_id(1) == 0)
        def _():
            wb_ref[...] = w_ref[...].astype(BF16)

        epilogue(_dot(x_ref[...], wb_ref[...]))
    else:
        (acc_ref,) = scratch
        k = pl.program_id(2)

        @pl.when(k == 0)
        def _():
            acc_ref[...] = jnp.zeros_like(acc_ref)

        acc_ref[...] += _dot(x_ref[...], w_ref[...].astype(BF16))

        @pl.when(k == nk - 1)
        def _():
            epilogue(acc_ref[...])


def _matmul(x, w, layer, rows, *, n_cols, col0=0, bm, bn_cap, bk=None, epi="f32", out_dtype=F32,
            res=None, mods=None, mod_layer=0, gate_col=0, name="mm"):
    t, kdim = x.shape
    bk = kdim if bk is None else bk
    bn = _pick_block(n_cols, bn_cap, col0, gate_col)
    assert t % bm == 0 and kdim % bk == 0
    nk = kdim // bk
    in_specs = [
        pl.BlockSpec((bm, bk), lambda n, m, k: (m, k)),
        pl.BlockSpec((None, bk, bn), lambda n, m, k: (layer, k, col0 // bn + n)),
    ]
    args = [x, w]
    if epi == "resid":
        assert gate_col % bn == 0
        in_specs.append(pl.BlockSpec((bm, bn), lambda n, m, k: (m, n)))
        in_specs.append(pl.BlockSpec((None, ROW_BLOCK, bn),
                                     lambda n, m, k: (mod_layer, rows.group_of_tile(m, bm), gate_col // bn + n)))
        args += [res, mods]
    scratch = [pltpu.VMEM((bk, bn), BF16)] if nk == 1 else [pltpu.VMEM((bm, bn), F32)]
    return pl.pallas_call(
        functools.partial(_mm_kernel, nk=nk, epi=epi),
        out_shape=jax.ShapeDtypeStruct((t, n_cols), out_dtype),
        grid=(n_cols // bn, t // bm, nk),
        in_specs=in_specs,
        out_specs=pl.BlockSpec((bm, bn), lambda n, m, k: (m, n)),
        scratch_shapes=scratch,
        compiler_params=_params(3),
        name=name,
    )(*args)


def _gelu_layernorm(v, g, b):
    v = jax.nn.gelu(v)
    mu = jnp.mean(v, axis=-1, keepdims=True)
    vc = v - mu
    var = jnp.mean(vc * vc, axis=-1, keepdims=True)
    return vc * lax.rsqrt(var + EPS) * g + b


def _even_prompt_kernel(u_ref, v_ref, bg_ref, cg_ref, xi_ref, lng_ref, lnb_ref, ws_ref, bse_ref, scw_ref,
                        y_ref, tail_ref, carry_ref, *, chunks_per_seq, n_groups, gd, d_a, taps):
    j = pl.program_id(0)

    @pl.when(j % chunks_per_seq == 0)
    def _():
        carry_ref[...] = jnp.zeros_like(carry_ref)

    u = jax.nn.gelu(u_ref[...])
    vn = _gelu_layernorm(v_ref[...], lng_ref[...], lnb_ref[...])
    tri = (lax.broadcasted_iota(jnp.int32, (ROW_BLOCK, ROW_BLOCK), 0)
           >= lax.broadcasted_iota(jnp.int32, (ROW_BLOCK, ROW_BLOCK), 1))
    for g in range(n_groups):
        sl = slice(g * gd, (g + 1) * gd)
        wg = jnp.where(tri, ws_ref[g], 0.0).astype(BF16)
        s = _dot(wg, vn[:, sl].astype(BF16)) + bse_ref[:, sl]
        y_ref[:, sl] = (u[:, sl] * s).astype(y_ref.dtype)

    cx = cg_ref[...] * xi_ref[...]
    row = lax.broadcasted_iota(jnp.int32, cx.shape, 0)
    conv = cx * scw_ref[taps - 1:taps, :]
    for d in range(1, taps):
        shifted = pltpu.roll(cx, d, axis=0)
        for r in range(d):
            src = SUBLANES - d + r
            shifted = jnp.where(row == r, carry_ref[src:src + 1, :], shifted)
        conv = conv + shifted * scw_ref[taps - 1 - d:taps - d, :]
    y_ref[:, d_a:] = (bg_ref[...] * conv).astype(y_ref.dtype)
    last = cx[ROW_BLOCK - SUBLANES:, :]
    carry_ref[...] = last
    tail_ref[...] = last


def _even_sample_kernel(wsm_ref, bsm_ref, u_ref, v_ref, bg_ref, cg_ref, xi_ref, *rest,
                        n_steps, n_groups, gd, d_a, taps):
    st_refs = rest[:taps - 1]
    lng_ref, lnb_ref, scw_ref, _, y_ref, vout_ref, scn_ref, vbuf_ref, car_ref = rest[taps - 1:]
    t = pl.program_id(0)

    @pl.when(t == 0)
    def _():
        vbuf_ref[...] = jnp.zeros_like(vbuf_ref)
        for k in range(taps - 1):
            car_ref[k] = st_refs[k][...]

    u = jax.nn.gelu(u_ref[...])
    vn = _gelu_layernorm(v_ref[...], lng_ref[...], lnb_ref[...])
    vout_ref[...] = vn
    vbuf_ref[t] = vn
    for g in range(n_groups):
        sl = slice(g * gd, (g + 1) * gd)
        acc = jnp.zeros((ROW_BLOCK, gd), F32)
        for s in range(n_steps):
            coef = jnp.where(s <= t, wsm_ref[(g * n_steps + t) * n_steps + s], 0.0)
            acc = acc + coef * vbuf_ref[s, :, sl]
        sg = acc + bsm_ref[g * n_steps + t]
        y_ref[:, sl] = (u[:, sl] * sg).astype(y_ref.dtype)

    cx = cg_ref[...] * xi_ref[...]
    conv = cx * scw_ref[taps - 1:taps, :]
    for k in range(taps - 1):
        conv = conv + car_ref[k] * scw_ref[k:k + 1, :]
    y_ref[:, d_a:] = (bg_ref[...] * conv).astype(y_ref.dtype)
    scn_ref[...] = cx
    for k in range(taps - 2):
        car_ref[k] = car_ref[k + 1]
    car_ref[taps - 2] = cx


def _even_mixer(proj, rows, sc_state, ln_g, ln_b, w_s, b_s, sc_w):
    n_groups, chunk, _ = w_s.shape
    d_a = ln_g.shape[0]
    taps, d_b = sc_w.shape
    gd = d_a // n_groups
    assert chunk == ROW_BLOCK and d_a == d_b and gd % 128 == 0 and taps - 1 <= SUBLANES
    bw = d_a
    t, n_dec, n_steps = rows.t, rows.n_dec, rows.dec_seq
    assert n_steps <= chunk and n_steps >= taps - 1
    col_specs = lambda row_of: [pl.BlockSpec((ROW_BLOCK, bw), lambda j, c=c: (row_of(j), c)) for c in range(5)]
    const = lambda shape: pl.BlockSpec(shape, lambda j: (0,) * len(shape))

    bse = jnp.repeat(b_s.T, gd, axis=1)
    y, tail = pl.pallas_call(
        functools.partial(_even_prompt_kernel, chunks_per_seq=rows.chunks_per_seq, n_groups=n_groups,
                          gd=gd, d_a=d_a, taps=taps),
        out_shape=(jax.ShapeDtypeStruct((t, d_a + d_b), BF16),
                   jax.ShapeDtypeStruct((rows.n_prompt, SUBLANES, d_b), F32)),
        grid=(rows.prompt_blocks,),
        in_specs=col_specs(lambda j: j) + [
            const((1, d_a)), const((1, d_a)), const((n_groups, chunk, chunk)), const((chunk, d_a)),
            const((taps, d_b))],
        out_specs=(pl.BlockSpec((ROW_BLOCK, d_a + d_b), lambda j: (j, 0)),
                   pl.BlockSpec((None, SUBLANES, d_b), lambda j: (j // rows.chunks_per_seq, 0, 0))),
        scratch_shapes=[pltpu.VMEM((SUBLANES, d_b), F32)],
        compiler_params=_params(1),
        name="even_mixer_prompt",
    )(proj, proj, proj, proj, proj, ln_g.reshape(1, d_a), ln_b.reshape(1, d_a), w_s, bse, sc_w)

    smem = pl.BlockSpec(memory_space=pltpu.SMEM)
    state2d = sc_state.reshape(n_dec, (taps - 1) * d_b)
    first_new = n_steps - (taps - 1)
    y, v_new, sc_new = pl.pallas_call(
        functools.partial(_even_sample_kernel, n_steps=n_steps, n_groups=n_groups, gd=gd, d_a=d_a, taps=taps),
        out_shape=(jax.ShapeDtypeStruct((t, d_a + d_b), BF16),
                   jax.ShapeDtypeStruct((n_dec, n_steps * d_a), F32),
                   jax.ShapeDtypeStruct((n_dec, (taps - 1) * d_b), F32)),
        grid=(n_steps,),
        in_specs=[smem, smem] + col_specs(lambda j: rows.prompt_blocks + j)
        + [pl.BlockSpec((n_dec, d_b), lambda j, k=k: (0, k)) for k in range(taps - 1)]
        + [const((1, d_a)), const((1, d_a)), const((taps, d_b)), pl.BlockSpec(memory_space=pl.ANY)],
        out_specs=(pl.BlockSpec((ROW_BLOCK, d_a + d_b), lambda j: (rows.prompt_blocks + j, 0)),
                   pl.BlockSpec((n_dec, d_a), lambda j: (0, j)),
                   pl.BlockSpec((n_dec, d_b), lambda j: (0, jnp.maximum(j - first_new, 0)))),
        scratch_shapes=[pltpu.VMEM((n_steps, n_dec, d_a), F32), pltpu.VMEM((taps - 1, n_dec, d_b), F32)],
        input_output_aliases={9 + taps: 0},
        compiler_params=_params(1),
        name="even_mixer_decode",
    )(w_s[:, :n_steps, :n_steps].reshape(-1), b_s[:, :n_steps].reshape(-1), proj, proj, proj, proj, proj,
      *([state2d] * (taps - 1)), ln_g.reshape(1, d_a), ln_b.reshape(1, d_a), sc_w, y)
    return (y, tail[:, SUBLANES - (taps - 1):, :], sc_new.reshape(n_dec, taps - 1, d_b),
            v_new.reshape(n_dec, n_steps, d_a))


def _silu(x):
    return x * jax.nn.sigmoid(x)


def _mpre_prompt_kernel(x_ref, w_ref, b_ref, o_ref, tail_ref, carry_ref, *, tiles_per_seq, taps):
    j = pl.program_id(1)

    @pl.when(j % tiles_per_seq == 0)
    def _():
        carry_ref[...] = jnp.zeros_like(carry_ref)

    x = x_ref[...]
    row = lax.broadcasted_iota(jnp.int32, x.shape, 0)
    acc = x * w_ref[taps - 1:taps, :]
    for d in range(1, taps):
        shifted = pltpu.roll(x, d, axis=0)
        for r in range(d):
            src = SUBLANES - d + r
            shifted = jnp.where(row == r, carry_ref[src:src + 1, :], shifted)
        acc = acc + shifted * w_ref[taps - 1 - d:taps - d, :]
    o_ref[...] = _silu(acc + b_ref[...])
    last = x[x.shape[0] - SUBLANES:, :]
    carry_ref[...] = last
    tail_ref[...] = last


def _mpre_sample_kernel(x_ref, *rest, taps):
    st_refs = rest[:taps - 1]
    w_ref, b_ref, o_ref, new_ref, car_ref = rest[taps - 1:]
    t = pl.program_id(1)

    @pl.when(t == 0)
    def _():
        for k in range(taps - 1):
            car_ref[k] = st_refs[k][...]

    x = x_ref[...]
    acc = x * w_ref[taps - 1:taps, :]
    for k in range(taps - 1):
        acc = acc + car_ref[k] * w_ref[k:k + 1, :]
    o_ref[...] = _silu(acc + b_ref[...])
    new_ref[...] = x
    for k in range(taps - 2):
        car_ref[k] = car_ref[k + 1]
    car_ref[taps - 2] = x


def _mamba_pre(zx, rows, xbc_col0, conv_state, conv_w, conv_b):
    taps, conv_dim = conv_w.shape
    bw = _pick_block(conv_dim, 2048, xbc_col0)
    assert taps - 1 <= SUBLANES
    ncb = conv_dim // bw
    cb0 = xbc_col0 // bw
    n_dec, n_steps = rows.n_dec, rows.dec_seq
    wspec = pl.BlockSpec((taps, bw), lambda c, j: (0, c))
    bspec = pl.BlockSpec((1, bw), lambda c, j: (0, c))
    rt = rows.tile_rows(512)
    tiles_per_seq = rows.seq // rt
    xp, tail = pl.pallas_call(
        functools.partial(_mpre_prompt_kernel, tiles_per_seq=tiles_per_seq, taps=taps),
        out_shape=(jax.ShapeDtypeStruct((rows.tp, conv_dim), F32),
                   jax.ShapeDtypeStruct((rows.n_prompt, SUBLANES, conv_dim), F32)),
        grid=(ncb, rows.tp // rt),
        in_specs=[pl.BlockSpec((rt, bw), lambda c, j: (j, cb0 + c)), wspec, bspec],
        out_specs=(pl.BlockSpec((rt, bw), lambda c, j: (j, c)),
                   pl.BlockSpec((None, SUBLANES, bw), lambda c, j: (j // tiles_per_seq, 0, c))),
        scratch_shapes=[pltpu.VMEM((SUBLANES, bw), F32)],
        compiler_params=_params(2),
        name="mamba_conv_prompt",
    )(zx, conv_w, conv_b.reshape(1, conv_dim))

    state2d = conv_state.reshape(n_dec, (taps - 1) * conv_dim)
    first_new = n_steps - (taps - 1)
    xs, new = pl.pallas_call(
        functools.partial(_mpre_sample_kernel, taps=taps),
        out_shape=(jax.ShapeDtypeStruct((n_dec, n_steps * conv_dim), F32),
                   jax.ShapeDtypeStruct((n_dec, (taps - 1) * conv_dim), F32)),
        grid=(ncb, n_steps),
        in_specs=[pl.BlockSpec((ROW_BLOCK, bw), lambda c, j: (rows.prompt_blocks + j, cb0 + c))]
        + [pl.BlockSpec((n_dec, bw), lambda c, j, k=k: (0, k * ncb + c)) for k in range(taps - 1)]
        + [wspec, bspec],
        out_specs=(pl.BlockSpec((n_dec, bw), lambda c, j: (0, j * ncb + c)),
                   pl.BlockSpec((n_dec, bw), lambda c, j: (0, jnp.maximum(j - first_new, 0) * ncb + c))),
        scratch_shapes=[pltpu.VMEM((taps - 1, n_dec, bw), F32)],
        compiler_params=_params(2),
        name="mamba_conv_decode",
    )(zx, *([state2d] * (taps - 1)), conv_w, conv_b.reshape(1, conv_dim))
    return (xp, xs.reshape(n_dec * n_steps, conv_dim), tail[:, SUBLANES - (taps - 1):, :],
            new.reshape(n_dec, taps - 1, conv_dim))


def _copy_kernel(x_ref, o_ref):
    o_ref[...] = x_ref[...].astype(o_ref.dtype)


def _steps_to_batch_major(x, rows, col0, n_cols, bw):
    assert n_cols % bw == 0 and col0 % bw == 0
    nc = n_cols // bw
    out = pl.pallas_call(
        _copy_kernel,
        out_shape=jax.ShapeDtypeStruct((rows.n_dec, rows.dec_seq * n_cols), F32),
        grid=(rows.dec_seq, nc),
        in_specs=[pl.BlockSpec((ROW_BLOCK, bw), lambda s, c: (rows.prompt_blocks + s, col0 // bw + c))],
        out_specs=pl.BlockSpec((rows.n_dec, bw), lambda s, c: (0, s * nc + c)),
        compiler_params=_params(2),
        name="to_batch_major",
    )(x)
    return out.reshape(rows.n_dec * rows.dec_seq, n_cols)


def _batch_to_steps_major(y_bt, into, rows, bw):
    c = y_bt.shape[1]
    assert c % bw == 0
    nc = c // bw
    return pl.pallas_call(
        lambda x_ref, _, o_ref: _copy_kernel(x_ref, o_ref),
        out_shape=jax.ShapeDtypeStruct(into.shape, into.dtype),
        grid=(rows.dec_seq, nc),
        in_specs=[pl.BlockSpec((rows.n_dec, bw), lambda s, j: (0, s * nc + j)),
                  pl.BlockSpec(memory_space=pl.ANY)],
        out_specs=pl.BlockSpec((ROW_BLOCK, bw), lambda s, j: (rows.prompt_blocks + s, j)),
        input_output_aliases={1: 0},
        compiler_params=_params(2),
        name="to_steps_major",
    )(y_bt.reshape(rows.n_dec, rows.dec_seq * c), into)


LOG2E = 1.4426950408889634


def _cumsum_rows(a):
    row = lax.broadcasted_iota(jnp.int32, a.shape, 0)
    shift = 1
    while shift < a.shape[0]:
        a = a + jnp.where(row >= shift, pltpu.roll(a, shift, axis=0), 0.0)
        shift *= 2
    return a


def _gate_norm(y, x, z, dsk, ng):
    yg = (y + dsk * x) * _silu(z)
    return yg * lax.rsqrt(jnp.mean(yg * yg, axis=-1, keepdims=True) + EPS) * ng


def _ssd_prompt_kernel(x_ref, b_ref, c_ref, dt_ref, z_ref, dtb_ref, alog_ref, dsk_ref, ng_ref, *rest,
                       heads, hd, ns, groups):
    y_ref, s_ref, xst_ref, yb_ref = rest[-4:]
    L = ROW_BLOCK
    hpg = heads // groups
    gw = hpg * hd
    hpt = LANES // hd

    @pl.when(pl.program_id(1) == 0)
    def _():
        s_ref[...] = jnp.zeros_like(s_ref)

    dt = jax.nn.softplus(dt_ref[...] + dtb_ref[...])
    cum2 = _cumsum_rows(dt * (-jnp.exp(alog_ref[...]) * LOG2E))
    cum2_t = cum2.T
    dt_t = dt.T
    crow = cum2_t - jnp.log2(dt_t)
    ri = lax.broadcasted_iota(jnp.int32, (L, L), 0)
    li = lax.broadcasted_iota(jnp.int32, (L, L), 1)
    tri = ri >= li
    lane_head = li // hd

    for g in range(groups):
        bg = b_ref[:, g * ns:(g + 1) * ns].astype(BF16)
        cg = c_ref[:, g * ns:(g + 1) * ns].astype(BF16)
        cb = _dot_nt(cg, bg)
        s_g = s_ref[g * hpg:(g + 1) * hpg].reshape(gw, ns)
        cs = _dot_nt(cg, s_g.astype(BF16))
        for j in range(gw // LANES):
            lanes = slice(g * gw + j * LANES, g * gw + (j + 1) * LANES)
            xp = x_ref[:, lanes]
            ws, blocks, cum_e = [], [], None
            for k in range(hpt):
                h = g * hpg + j * hpt + k
                ccol = jnp.broadcast_to(cum2[:, h:h + 1], (L, L))
                ws.append(jnp.where(tri, cb * jnp.exp2(ccol - crow[h:h + 1, :]), 0.0).astype(BF16))
                blocks.append(jnp.where(lane_head == k, xp, 0.0).astype(BF16))
                cum_e = ccol if k == 0 else jnp.where(lane_head >= k, ccol, cum_e)
            intra = _dot(jnp.concatenate(ws, axis=1), jnp.concatenate(blocks, axis=0))
            yb_ref[:, j * LANES:(j + 1) * LANES] = intra + cs[:, j * LANES:(j + 1) * LANES] * jnp.exp2(cum_e)

        x_t = x_ref[:, g * gw:(g + 1) * gw].T
        for r in range(hpg):
            h = g * hpg + r
            tail = jnp.exp2(cum2_t[h:h + 1, L - 1:L] - cum2_t[h:h + 1, :]) * dt_t[h:h + 1, :]
            xst_ref[r * hd:(r + 1) * hd, :] = (x_t[r * hd:(r + 1) * hd, :] * tail).astype(BF16)
        upd = _dot(xst_ref[...], bg)
        for r in range(hpg):
            h = g * hpg + r
            dec = jnp.exp2(jnp.broadcast_to(cum2[L - 1:L, h:h + 1], (hd, ns)))
            s_ref[h] = s_g[r * hd:(r + 1) * hd, :] * dec + upd[r * hd:(r + 1) * hd, :]

        sl = slice(g * gw, (g + 1) * gw)
        y_ref[:, sl] = _gate_norm(yb_ref[...], x_ref[:, sl], z_ref[:, sl], dsk_ref[:, sl],
                                  ng_ref[:, sl]).astype(y_ref.dtype)


def _ssd_decode_kernel(x_ref, b_ref, c_ref, dt_ref, z_ref, s0_ref, dtb_ref, alog_ref, dsk_ref, ng_ref, *rest,
                       n_steps, heads, hd, ns, groups):
    y_ref, s_ref = rest[-2:]
    T = n_steps
    hpg = heads // groups
    gw = hpg * hd
    hpt = LANES // hd
    d_inner = heads * hd

    dt = jax.nn.softplus(dt_ref[...] + dtb_ref[...])
    cum2 = _cumsum_rows(dt * (-jnp.exp(alog_ref[...]) * LOG2E))
    lane_head = lax.broadcasted_iota(jnp.int32, (T, LANES), 1) // hd
    cum_tiles, dt_tiles = [], []
    for j in range(d_inner // LANES):
        ce = de = None
        for k in range(hpt):
            h = j * hpt + k
            cc = jnp.broadcast_to(cum2[:, h:h + 1], (T, LANES))
            dd = jnp.broadcast_to(dt[:, h:h + 1], (T, LANES))
            ce = cc if k == 0 else jnp.where(lane_head >= k, cc, ce)
            de = dd if k == 0 else jnp.where(lane_head >= k, dd, de)
        cum_tiles.append(ce)
        dt_tiles.append(de)
    cum_e = jnp.concatenate(cum_tiles, axis=1)
    dt_e = jnp.concatenate(dt_tiles, axis=1)

    x = x_ref[...]
    xdt = x * dt_e
    xs = x * (jnp.exp2(cum_e[T - 1:T, :] - cum_e) * dt_e)
    step = lax.broadcasted_iota(jnp.int32, (T, 1), 0)
    acc = jnp.zeros((T, d_inner), F32)
    for s in range(T):
        e = jnp.exp2(jnp.minimum(cum_e - cum_e[s:s + 1, :], 0.0)) * xdt[s:s + 1, :]
        parts = []
        for g in range(groups):
            cb = jnp.sum(c_ref[:, g * ns:(g + 1) * ns] * b_ref[s:s + 1, g * ns:(g + 1) * ns], axis=-1, keepdims=True)
            parts.append(e[:, g * gw:(g + 1) * gw] * jnp.where(step >= s, cb, 0.0))
        acc = acc + jnp.concatenate(parts, axis=1)

    ecum = jnp.exp2(cum_e)
    pad = jnp.zeros((2 * SUBLANES - T, ns), F32)
    for g in range(groups):
        sl = slice(g * gw, (g + 1) * gw)
        s_g = s0_ref[g * hpg:(g + 1) * hpg].reshape(gw, ns)
        c16 = jnp.concatenate([c_ref[:, g * ns:(g + 1) * ns], pad], axis=0).astype(BF16)
        cs = _dot_nt(c16, s_g.astype(BF16))[:T]
        upd = lax.dot_general(xs[:, sl], b_ref[:, g * ns:(g + 1) * ns], (((0,), (0,)), ((), ())),
                              preferred_element_type=F32)
        for r in range(hpg):
            h = g * hpg + r
            dec = jnp.exp2(jnp.broadcast_to(cum2[T - 1:T, h:h + 1], (hd, ns)))
            s_ref[h] = s_g[r * hd:(r + 1) * hd, :] * dec + upd[r * hd:(r + 1) * hd, :]
        y_ref[:, sl] = _gate_norm(acc[:, sl] + cs * ecum[:, sl], x[:, sl], z_ref[:, sl], dsk_ref[:, sl],
                                  ng_ref[:, sl])


def _ssd_consts(dt_bias, a_log, d_skip, norm_g, hd):
    heads, d_inner = a_log.shape[0], norm_g.shape[0]
    return (dt_bias.reshape(1, heads), a_log.reshape(1, heads), jnp.repeat(d_skip, hd).reshape(1, d_inner),
            norm_g.reshape(1, d_inner))


def _ssd_prompt(xbc, dtraw, zx, consts, rows, layer, n_layers, prev_state, *, heads, hd, ns, groups):
    d_inner, gn = heads * hd, groups * ns
    cps = rows.chunks_per_seq
    row = lambda b, c: b * cps + c
    const = lambda width: pl.BlockSpec((1, width), lambda b, c: (0, 0))
    extra = [] if prev_state is None else [prev_state]
    n_in = 9 + len(extra)
    return pl.pallas_call(
        functools.partial(_ssd_prompt_kernel, heads=heads, hd=hd, ns=ns, groups=groups),
        out_shape=(jax.ShapeDtypeStruct((rows.t, d_inner), BF16),
                   jax.ShapeDtypeStruct((n_layers, rows.n_prompt, heads, hd, ns), F32)),
        grid=(rows.n_prompt, cps),
        in_specs=[
            pl.BlockSpec((ROW_BLOCK, d_inner), lambda b, c: (row(b, c), 0)),
            pl.BlockSpec((ROW_BLOCK, gn), lambda b, c: (row(b, c), d_inner // gn)),
            pl.BlockSpec((ROW_BLOCK, gn), lambda b, c: (row(b, c), d_inner // gn + 1)),
            pl.BlockSpec((ROW_BLOCK, heads), lambda b, c: (row(b, c), 0)),
            pl.BlockSpec((ROW_BLOCK, d_inner), lambda b, c: (row(b, c), 0)),
            const(heads), const(heads), const(d_inner), const(d_inner),
        ] + [pl.BlockSpec(memory_space=pl.ANY)] * len(extra),
        out_specs=(pl.BlockSpec((ROW_BLOCK, d_inner), lambda b, c: (row(b, c), 0)),
                   pl.BlockSpec((None, None, heads, hd, ns), lambda b, c: (layer, b, 0, 0, 0))),
        scratch_shapes=[pltpu.VMEM((d_inner // groups, ROW_BLOCK), BF16),
                        pltpu.VMEM((ROW_BLOCK, d_inner // groups), F32)],
        input_output_aliases={} if prev_state is None else {n_in - 1: 1},
        compiler_params=_params(2),
        name="ssd_prompt",
    )(xbc, xbc, xbc, dtraw, zx, *consts, *extra)


def _ssd_decode(xbc, dtraw, z, state, consts, rows, layer, prev_state, *, heads, hd, ns, groups):
    d_inner, gn = heads * hd, groups * ns
    n = rows.dec_seq
    const = lambda width: pl.BlockSpec((1, width), lambda b: (0, 0))
    extra = [] if prev_state is None else [prev_state]
    n_in = 10 + len(extra)
    state_spec = pl.BlockSpec((None, None, heads, hd, ns), lambda b: (layer, b, 0, 0, 0))
    return pl.pallas_call(
        functools.partial(_ssd_decode_kernel, n_steps=n, heads=heads, hd=hd, ns=ns, groups=groups),
        out_shape=(jax.ShapeDtypeStruct((rows.ts, d_inner), F32),
                   jax.ShapeDtypeStruct(state.shape, F32)),
        grid=(rows.n_dec,),
        in_specs=[
            pl.BlockSpec((n, d_inner), lambda b: (b, 0)),
            pl.BlockSpec((n, gn), lambda b: (b, d_inner // gn)),
            pl.BlockSpec((n, gn), lambda b: (b, d_inner // gn + 1)),
            pl.BlockSpec((n, heads), lambda b: (b, 0)),
            pl.BlockSpec((n, d_inner), lambda b: (b, 0)),
            state_spec,
            const(heads), const(heads), const(d_inner), const(d_inner),
        ] + [pl.BlockSpec(memory_space=pl.ANY)] * len(extra),
        out_specs=(pl.BlockSpec((n, d_inner), lambda b: (b, 0)), state_spec),
        input_output_aliases={} if prev_state is None else {n_in - 1: 1},
        compiler_params=_params(1),
        name="ssd_decode",
    )(xbc, xbc, xbc, dtraw, z, state, *consts, *extra)


def kernel(x_prompt, x_sample, state_shortconv, state_mamba_conv, state_ssm, c_prompt, c_sample, w_ada, b_ada, g_mix, g_ffn, ev_w_in, ev_ln_g, ev_ln_b, ev_w_s, ev_b_s, ev_sc_w, ev_w_out, m_w_in, m_conv_w, m_conv_b, m_dt_bias, m_a_log, m_d, m_norm_g, m_w_out, ff_w1, ff_w2, g_final):
    n_prompt, seq, d = x_prompt.shape
    n_dec, dec_seq, _ = x_sample.shape
    rows = _Rows(n_prompt, seq, n_dec, dec_seq)
    depth = w_ada.shape[0]
    d_ff = ff_w1.shape[2]
    d_a = ev_ln_g.shape[1]
    d_b = ev_sc_w.shape[2]
    heads = m_a_log.shape[1]
    d_inner = m_norm_g.shape[1]
    conv_dim = m_conv_w.shape[2]
    ns = state_ssm.shape[-1]
    groups = (conv_dim - d_inner) // (2 * ns)
    gn = groups * ns
    assert d_inner % gn == 0 and m_w_in.shape[2] == 2 * d_inner + 2 * gn + heads

    bm = rows.tile_rows(512)
    bm_k = rows.tile_rows(1024)
    lane_block = _pick_block(d_inner, 2048)

    x = jnp.concatenate([x_prompt.reshape(rows.tp, d),
                         jnp.swapaxes(x_sample, 0, 1).reshape(rows.ts, d)], axis=0)
    mods = _ada_all(c_prompt, c_sample, w_ada, b_ada)

    sc_p, sc_s, v_s, mc_p, mc_s = [], [], [], [], []
    ssm_p = ssm_s = None
    n_odd = state_ssm.shape[0]
    hd = d_inner // heads
    for i in range(depth):
        hm = _norm_mod(x, g_mix[i], rows, mods, i, shift_col=0, scale_col=1, name="norm_mix")
        if i % 2 == 0:
            e = i // 2
            n_proj = ev_w_in.shape[2]
            proj = _matmul_stationary(hm, ev_w_in, e, rows, n_cols=n_proj, bm=bm_k, bn_cap=1024, name="even_in")
            ymix, scp, scs, vs = _even_mixer(proj, rows, state_shortconv[e], ev_ln_g[e], ev_ln_b[e],
                                             ev_w_s[e], ev_b_s[e], ev_sc_w[e])
            sc_p.append(scp)
            sc_s.append(scs)
            v_s.append(vs)
            x = _matmul_stationary(ymix, ev_w_out, e, rows, n_cols=d, bm=bm_k, bn_cap=512, epi="resid", res=x,
                                   mods=mods, mod_layer=i, gate_col=2 * d, name="even_out")
        else:
            o = i // 2
            n_zx = 2 * d_inner + 2 * gn
            zx = _matmul_stationary(hm, m_w_in, o, rows, n_cols=n_zx, bm=bm_k, bn_cap=1024, name="mamba_in")
            dtraw = _matmul(hm, m_w_in, o, rows, n_cols=heads, col0=n_zx, bm=bm, bn_cap=128, name="mamba_dt")
            xp, xs, mcp, mcs = _mamba_pre(zx, rows, d_inner, state_mamba_conv[o], m_conv_w[o], m_conv_b[o])
            mc_p.append(mcp)
            mc_s.append(mcs)
            consts = _ssd_consts(m_dt_bias[o], m_a_log[o], m_d[o], m_norm_g[o], hd)
            dims = dict(heads=heads, hd=hd, ns=ns, groups=groups)
            ymix, ssm_p = _ssd_prompt(xp, dtraw, zx, consts, rows, o, n_odd, ssm_p, **dims)
            z_s = _steps_to_batch_major(zx, rows, 0, d_inner, lane_block)
            dt_s = _steps_to_batch_major(dtraw, rows, 0, heads, heads)
            ys, ssm_s = _ssd_decode(xs, dt_s, z_s, state_ssm, consts, rows, o, ssm_s, **dims)
            ymix = _batch_to_steps_major(ys, ymix, rows, lane_block)
            x = _matmul_stationary(ymix, m_w_out, o, rows, n_cols=d, bm=bm, bn_cap=512, epi="resid", res=x,
                                   mods=mods, mod_layer=i, gate_col=2 * d, name="mamba_out")
        hf = _norm_mod(x, g_ffn[i], rows, mods, i, shift_col=3, scale_col=4, name="norm_ffn")
        a2 = _matmul_stationary(hf, ff_w1, i, rows, n_cols=d_ff, bm=bm_k, bn_cap=1024, epi="relu2",
                                out_dtype=BF16, name="ffn_up")
        x = _matmul(a2, ff_w2, i, rows, n_cols=d, bm=bm_k, bn_cap=1024, bk=d_ff // 8, epi="resid", res=x,
                    mods=mods, mod_layer=i, gate_col=5 * d, name="ffn_down")

    y_prompt, y_sample = _norm_final(x, g_final, rows)
    return (y_prompt, y_sample, jnp.stack(sc_p), jnp.stack(sc_s), jnp.stack(v_s), jnp.stack(mc_p),
            jnp.stack(mc_s), ssm_p, ssm_s)
```

```python
import functools

import jax
import jax.numpy as jnp
from jax import lax
from jax.experimental import pallas as pl
from jax.experimental.pallas import tpu as pltpu

EPS = 1e-5
ROW_BLOCK = 128
SUBLANES = 8
LANES = 128
V7X_VMEM_LIMIT = 58 * 1024 * 1024
BF16 = jnp.bfloat16
F32 = jnp.float32


def _params(n_axes, vmem_mb=None):
    limit = V7X_VMEM_LIMIT if vmem_mb is None else min(vmem_mb * 1024 * 1024, V7X_VMEM_LIMIT)
    return pltpu.CompilerParams(dimension_semantics=("arbitrary",) * n_axes, vmem_limit_bytes=limit)


def _pick_block(n, cap, *also):
    for c in (2048, 1024, 512, 256, 128):
        if c <= cap and n % c == 0 and all(a % c == 0 for a in also):
            return c
    raise ValueError(f"no lane-aligned block divides {n}")


def _dot(a, b):
    return jnp.dot(a, b, preferred_element_type=F32)


def _dot_nt(a, b):
    return lax.dot_general(a, b, (((1,), (1,)), ((), ())), preferred_element_type=F32)


def _ada_kernel(c_ref, w_ref, b_ref, o_ref, cact_ref, *, n_prompt, n_dec):
    first = jnp.logical_and(pl.program_id(0) == 0, pl.program_id(1) == 0)

    @pl.when(first)
    def _():
        c = c_ref[...]
        cact_ref[...] = (c * jax.nn.sigmoid(c)).astype(BF16)

    acc = _dot(cact_ref[...], w_ref[...].astype(BF16)) + b_ref[...]
    for b in range(n_prompt):
        row = acc[n_dec + b:n_dec + b + 1, :]
        o_ref[b * ROW_BLOCK:(b + 1) * ROW_BLOCK, :] = jnp.broadcast_to(row, (ROW_BLOCK, acc.shape[1]))
    o_ref[n_prompt * ROW_BLOCK:, :] = acc[:n_dec, :]


def _ada_all(c_prompt, c_sample, w_ada, b_ada, bn=1024):
    depth, d, n6 = w_ada.shape
    n_prompt, n_dec = c_prompt.shape[0], c_sample.shape[0]
    assert n_dec == ROW_BLOCK and n6 % bn == 0
    pad = (-(n_dec + n_prompt)) % SUBLANES
    c_rows = jnp.concatenate([c_sample, c_prompt, jnp.zeros((pad, d), c_prompt.dtype)], axis=0)
    rows = c_rows.shape[0]
    n_groups = n_prompt + 1
    return pl.pallas_call(
        functools.partial(_ada_kernel, n_prompt=n_prompt, n_dec=n_dec),
        out_shape=jax.ShapeDtypeStruct((depth, n_groups * ROW_BLOCK, n6), F32),
        grid=(depth, n6 // bn),
        in_specs=[
            pl.BlockSpec((rows, d), lambda l, n: (0, 0)),
            pl.BlockSpec((None, d, bn), lambda l, n: (l, 0, n)),
            pl.BlockSpec((None, 1, bn), lambda l, n: (l, 0, n)),
        ],
        out_specs=pl.BlockSpec((None, n_groups * ROW_BLOCK, bn), lambda l, n: (l, 0, n)),
        scratch_shapes=[pltpu.VMEM((rows, d), BF16)],
        compiler_params=_params(2),
        name="ada_mod",
    )(c_rows, w_ada, b_ada.reshape(depth, 1, n6))


class _Rows:
    def __init__(self, n_prompt, seq, n_dec, dec_seq):
        self.n_prompt, self.seq, self.n_dec, self.dec_seq = n_prompt, seq, n_dec, dec_seq
        self.tp = n_prompt * seq
        self.ts = n_dec * dec_seq
        self.t = self.tp + self.ts
        assert n_dec == ROW_BLOCK and seq % ROW_BLOCK == 0
        self.chunks_per_seq = seq // ROW_BLOCK
        self.prompt_blocks = self.tp // ROW_BLOCK

    def tile_rows(self, cap):
        bm = cap
        while self.seq % bm or self.ts % bm:
            bm //= 2
        assert bm >= ROW_BLOCK
        return bm

    def group_of_tile(self, i, bm):
        return jnp.where(i * bm < self.tp, (i * bm) // self.seq, self.n_prompt)


def _norm_kernel(*refs, modulated):
    if modulated:
        x_ref, g_ref, sh_ref, sc_ref, o_ref = refs
    else:
        x_ref, g_ref, o_ref = refs
    x = x_ref[...]
    y = x * lax.rsqrt(jnp.mean(x * x, axis=-1, keepdims=True) + EPS) * g_ref[...]
    if modulated:
        bm, d = y.shape
        y3 = y.reshape(bm // ROW_BLOCK, ROW_BLOCK, d)
        y = (y3 * (1.0 + sc_ref[...])[None] + sh_ref[...][None]).reshape(bm, d)
    o_ref[...] = y.astype(o_ref.dtype)


def _norm_mod(x, g, rows, mods=None, layer=0, shift_col=0, scale_col=0, out_dtype=BF16, name="norm"):
    t, d = x.shape
    bm = rows.tile_rows(512)
    in_specs = [pl.BlockSpec((bm, d), lambda i: (i, 0)), pl.BlockSpec((1, d), lambda i: (0, 0))]
    args = [x, g.reshape(1, d)]
    if mods is not None:
        for col in (shift_col, scale_col):
            in_specs.append(pl.BlockSpec((None, ROW_BLOCK, d),
                                         lambda i, col=col: (layer, rows.group_of_tile(i, bm), col)))
            args.append(mods)
    return pl.pallas_call(
        functools.partial(_norm_kernel, modulated=mods is not None),
        out_shape=jax.ShapeDtypeStruct((t, d), out_dtype),
        grid=(t // bm,),
        in_specs=in_specs,
        out_specs=pl.BlockSpec((bm, d), lambda i: (i, 0)),
        compiler_params=_params(1),
        name=name,
    )(*args)


def _norm_final(x, g, rows):
    _, d = x.shape
    bm = rows.tile_rows(512)
    body = functools.partial(_norm_kernel, modulated=False)
    g_spec = pl.BlockSpec((1, d), lambda i: (0, 0))
    y_prompt = pl.pallas_call(
        body,
        out_shape=jax.ShapeDtypeStruct((rows.tp, d), F32),
        grid=(rows.tp // bm,),
        in_specs=[pl.BlockSpec((bm, d), lambda i: (i, 0)), g_spec],
        out_specs=pl.BlockSpec((bm, d), lambda i: (i, 0)),
        compiler_params=_params(1),
        name="norm_final_prompt",
    )(x, g.reshape(1, d))
    y_decode = pl.pallas_call(
        body,
        out_shape=jax.ShapeDtypeStruct((rows.n_dec, rows.dec_seq * d), F32),
        grid=(rows.dec_seq,),
        in_specs=[pl.BlockSpec((ROW_BLOCK, d), lambda s: (rows.prompt_blocks + s, 0)), g_spec],
        out_specs=pl.BlockSpec((rows.n_dec, d), lambda s: (0, s)),
        compiler_params=_params(1),
        name="norm_final_decode",
    )(x, g.reshape(1, d))
    return (y_prompt.reshape(rows.n_prompt, rows.seq, d), y_decode.reshape(rows.n_dec, rows.dec_seq, d))


def _mm_epilogue(acc, epi, o_ref, res_ref, gate_ref):
    if epi == "f32":
        o_ref[...] = acc
    elif epi == "relu2":
        a = jnp.maximum(acc, 0.0)
        o_ref[...] = (a * a).astype(o_ref.dtype)
    else:
        bm, bn = acc.shape
        gated = acc.reshape(bm // ROW_BLOCK, ROW_BLOCK, bn) * gate_ref[...][None]
        o_ref[...] = res_ref[...] + gated.reshape(bm, bn)


def _mm_stationary_kernel(*refs, layer, col0, ck, n_chunks, per_step, epi):
    if epi == "resid":
        x_ref, w_hbm, res_ref, gate_ref, o_ref, wb_ref, stage_ref, sem = refs
    else:
        x_ref, w_hbm, o_ref, wb_ref, stage_ref, sem = refs
        res_ref = gate_ref = None
    n, m = pl.program_id(0), pl.program_id(1)
    bn = wb_ref.shape[2]
    slot = n % 2

    def chunk_copy(tile, chunk, buf):
        src = w_hbm.at[layer, pl.ds(pl.multiple_of(chunk * ck, ck), ck),
                       pl.ds(pl.multiple_of(col0 + tile * bn, LANES), bn)]
        return pltpu.make_async_copy(src, stage_ref.at[buf], sem.at[buf])

    @pl.when(jnp.logical_and(n == 0, m == 0))
    def _():
        chunk_copy(0, 0, 0).start()
        for c in range(n_chunks):
            if c + 1 < n_chunks:
                chunk_copy(0, c + 1, (c + 1) % 2).start()
            chunk_copy(0, c, c % 2).wait()
            wb_ref[0, c * ck:(c + 1) * ck, :] = stage_ref[c % 2].astype(BF16)

    def prefetch(action):
        for i in range(per_step):
            chunk = m * per_step + i

            @pl.when(jnp.logical_and(n + 1 < pl.num_programs(0), chunk < n_chunks))
            def _():
                action(chunk, i)

    prefetch(lambda chunk, buf: chunk_copy(n + 1, chunk, buf).start())
    _mm_epilogue(_dot(x_ref[...], wb_ref[slot]), epi, o_ref, res_ref, gate_ref)

    def land(chunk, buf):
        chunk_copy(n + 1, chunk, buf).wait()
        wb_ref[1 - slot, pl.ds(pl.multiple_of(chunk * ck, ck), ck), :] = stage_ref[buf].astype(BF16)

    prefetch(land)


def _matmul_stationary(x, w, layer, rows, *, n_cols, col0=0, bm, bn_cap, ck=512, epi="f32", out_dtype=F32,
                       res=None, mods=None, mod_layer=0, gate_col=0, name="mm"):
    t, kdim = x.shape
    bn = _pick_block(n_cols, bn_cap, col0, gate_col)
    ck = min(ck, kdim)
    assert t % bm == 0 and kdim % ck == 0
    n_chunks = kdim // ck
    n_row_tiles = t // bm
    per_step = -(-n_chunks // n_row_tiles)
    in_specs = [pl.BlockSpec((bm, kdim), lambda n, m: (m, 0)), pl.BlockSpec(memory_space=pl.ANY)]
    args = [x, w]
    if epi == "resid":
        in_specs.append(pl.BlockSpec((bm, bn), lambda n, m: (m, n)))
        in_specs.append(pl.BlockSpec((None, ROW_BLOCK, bn),
                                     lambda n, m: (mod_layer, rows.group_of_tile(m, bm), gate_col // bn + n)))
        args += [res, mods]
    return pl.pallas_call(
        functools.partial(_mm_stationary_kernel, layer=layer, col0=col0, ck=ck, n_chunks=n_chunks,
                          per_step=per_step, epi=epi),
        out_shape=jax.ShapeDtypeStruct((t, n_cols), out_dtype),
        grid=(n_cols // bn, n_row_tiles),
        in_specs=in_specs,
        out_specs=pl.BlockSpec((bm, bn), lambda n, m: (m, n)),
        scratch_shapes=[pltpu.VMEM((2, kdim, bn), BF16), pltpu.VMEM((max(2, per_step), ck, bn), F32),
                        pltpu.SemaphoreType.DMA((max(2, per_step),))],
        compiler_params=_params(2),
        name=name,
    )(*args)


def _mm_kernel(*refs, nk, epi):
    if epi == "resid":
        x_ref, w_ref, res_ref, gate_ref, o_ref, *scratch = refs
    else:
        x_ref, w_ref, o_ref, *scratch = refs
        res_ref = gate_ref = None

    def epilogue(acc):
        _mm_epilogue(acc, epi, o_ref, res_ref, gate_ref)

    if nk == 1:
        (wb_ref,) = scratch

        @pl.when(pl.program_id(1) == 0)
        def _():
            wb_ref[...] = w_ref[...].astype(BF16)

        epilogue(_dot(x_ref[...], wb_ref[...]))
    else:
        (acc_ref,) = scratch
        k = pl.program_id(2)

        @pl.when(k == 0)
        def _():
            acc_ref[...] = jnp.zeros_like(acc_ref)

        acc_ref[...] += _dot(x_ref[...], w_ref[...].astype(BF16))

        @pl.when(k == nk - 1)
        def _():
            epilogue(acc_ref[...])


def _matmul(x, w, layer, rows, *, n_cols, col0=0, bm, bn_cap, bk=None, epi="f32", out_dtype=F32,
            res=None, mods=None, mod_layer=0, gate_col=0, name="mm"):
    t, kdim = x.shape
    bk = kdim if bk is None else bk
    bn = _pick_block(n_cols, bn_cap, col0, gate_col)
    assert t % bm == 0 and kdim % bk == 0
    nk = kdim // bk
    in_specs = [
        pl.BlockSpec((bm, bk), lambda n, m, k: (m, k)),
        pl.BlockSpec((None, bk, bn), lambda n, m, k: (layer, k, col0 // bn + n)),
    ]
    args = [x, w]
    if epi == "resid":
        assert gate_col % bn == 0
        in_specs.append(pl.BlockSpec((bm, bn), lambda n, m, k: (m, n)))
        in_specs.append(pl.BlockSpec((None, ROW_BLOCK, bn),
                                     lambda n, m, k: (mod_layer, rows.group_of_tile(m, bm), gate_col // bn + n)))
        args += [res, mods]
    scratch = [pltpu.VMEM((bk, bn), BF16)] if nk == 1 else [pltpu.VMEM((bm, bn), F32)]
    return pl.pallas_call(
        functools.partial(_mm_kernel, nk=nk, epi=epi),
        out_shape=jax.ShapeDtypeStruct((t, n_cols), out_dtype),
        grid=(n_cols // bn, t // bm, nk),
        in_specs=in_specs,
        out_specs=pl.BlockSpec((bm, bn), lambda n, m, k: (m, n)),
        scratch_shapes=scratch,
        compiler_params=_params(3),
        name=name,
    )(*args)


def _gelu_layernorm(v, g, b):
    v = jax.nn.gelu(v)
    mu = jnp.mean(v, axis=-1, keepdims=True)
    vc = v - mu
    var = jnp.mean(vc * vc, axis=-1, keepdims=True)
    return vc * lax.rsqrt(var + EPS) * g + b


def _even_prompt_kernel(u_ref, v_ref, bg_ref, cg_ref, xi_ref, lng_ref, lnb_ref, ws_ref, bse_ref, scw_ref,
                        y_ref, tail_ref, carry_ref, *, chunks_per_seq, n_groups, gd, d_a, taps):
    j = pl.program_id(0)

    @pl.when(j % chunks_per_seq == 0)
    def _():
        carry_ref[...] = jnp.zeros_like(carry_ref)

    u = jax.nn.gelu(u_ref[...])
    vn = _gelu_layernorm(v_ref[...], lng_ref[...], lnb_ref[...])
    tri = (lax.broadcasted_iota(jnp.int32, (ROW_BLOCK, ROW_BLOCK), 0)
           >= lax.broadcasted_iota(jnp.int32, (ROW_BLOCK, ROW_BLOCK), 1))
    for g in range(n_groups):
        sl = slice(g * gd, (g + 1) * gd)
        wg = jnp.where(tri, ws_ref[g], 0.0).astype(BF16)
        s = _dot(wg, vn[:, sl].astype(BF16)) + bse_ref[:, sl]
        y_ref[:, sl] = (u[:, sl] * s).astype(y_ref.dtype)

    cx = cg_ref[...] * xi_ref[...]
    row = lax.broadcasted_iota(jnp.int32, cx.shape, 0)
    conv = cx * scw_ref[taps - 1:taps, :]
    for d in range(1, taps):
        shifted = pltpu.roll(cx, d, axis=0)
        for r in range(d):
            src = SUBLANES - d + r
            shifted = jnp.where(row == r, carry_ref[src:src + 1, :], shifted)
        conv = conv + shifted * scw_ref[taps - 1 - d:taps - d, :]
    y_ref[:, d_a:] = (bg_ref[...] * conv).astype(y_ref.dtype)
    last = cx[ROW_BLOCK - SUBLANES:, :]
    carry_ref[...] = last
    tail_ref[...] = last


def _even_sample_kernel(wsm_ref, bsm_ref, u_ref, v_ref, bg_ref, cg_ref, xi_ref, st_ref, lng_ref, lnb_ref, scw_ref,
                        _, y_ref, vout_ref, scn_ref, vbuf_ref, car_ref, *, n_steps, n_groups, gd, d_a, taps):
    t = pl.program_id(0)

    @pl.when(t == 0)
    def _():
        vbuf_ref[...] = jnp.zeros_like(vbuf_ref)
        for k in range(taps - 1):
            car_ref[k] = st_ref[:, k, :]

    u = jax.nn.gelu(u_ref[...])
    vn = _gelu_layernorm(v_ref[...], lng_ref[...], lnb_ref[...])
    vout_ref[...] = vn
    vbuf_ref[t] = vn
    for g in range(n_groups):
        sl = slice(g * gd, (g + 1) * gd)
        acc = jnp.zeros((ROW_BLOCK, gd), F32)
        for s in range(n_steps):
            coef = jnp.where(s <= t, wsm_ref[(g * n_steps + t) * n_steps + s], 0.0)
            acc = acc + coef * vbuf_ref[s, :, sl]
        sg = acc + bsm_ref[g * n_steps + t]
        y_ref[:, sl] = (u[:, sl] * sg).astype(y_ref.dtype)

    cx = cg_ref[...] * xi_ref[...]
    conv = cx * scw_ref[taps - 1:taps, :]
    for k in range(taps - 1):
        conv = conv + car_ref[k] * scw_ref[k:k + 1, :]
    y_ref[:, d_a:] = (bg_ref[...] * conv).astype(y_ref.dtype)
    for k in range(taps - 1):
        @pl.when(t == n_steps - (taps - 1) + k)
        def _():
            scn_ref[:, k, :] = cx
    for k in range(taps - 2):
        car_ref[k] = car_ref[k + 1]
    car_ref[taps - 2] = cx


def _even_mixer(proj, rows, sc_states, layer, ln_g, ln_b, w_s, b_s, sc_w):
    n_groups, chunk, _ = w_s.shape
    d_a = ln_g.shape[0]
    taps, d_b = sc_w.shape
    gd = d_a // n_groups
    assert chunk == ROW_BLOCK and d_a == d_b and gd % 128 == 0 and taps - 1 <= SUBLANES
    bw = d_a
    t, n_dec, n_steps = rows.t, rows.n_dec, rows.dec_seq
    assert n_steps <= chunk and n_steps >= taps - 1
    col_specs = lambda row_of: [pl.BlockSpec((ROW_BLOCK, bw), lambda j, c=c: (row_of(j), c)) for c in range(5)]
    const = lambda shape: pl.BlockSpec(shape, lambda j: (0,) * len(shape))

    bse = jnp.repeat(b_s.T, gd, axis=1)
    y, tail = pl.pallas_call(
        functools.partial(_even_prompt_kernel, chunks_per_seq=rows.chunks_per_seq, n_groups=n_groups,
                          gd=gd, d_a=d_a, taps=taps),
        out_shape=(jax.ShapeDtypeStruct((t, d_a + d_b), BF16),
                   jax.ShapeDtypeStruct((rows.n_prompt, SUBLANES, d_b), F32)),
        grid=(rows.prompt_blocks,),
        in_specs=col_specs(lambda j: j) + [
            const((1, d_a)), const((1, d_a)), const((n_groups, chunk, chunk)), const((chunk, d_a)),
            const((taps, d_b))],
        out_specs=(pl.BlockSpec((ROW_BLOCK, d_a + d_b), lambda j: (j, 0)),
                   pl.BlockSpec((None, SUBLANES, d_b), lambda j: (j // rows.chunks_per_seq, 0, 0))),
        scratch_shapes=[pltpu.VMEM((SUBLANES, d_b), F32)],
        compiler_params=_params(1),
        name="even_mixer_prompt",
    )(proj, proj, proj, proj, proj, ln_g.reshape(1, d_a), ln_b.reshape(1, d_a), w_s, bse, sc_w)

    smem = pl.BlockSpec(memory_space=pltpu.SMEM)
    y, v_new, sc_new = pl.pallas_call(
        functools.partial(_even_sample_kernel, n_steps=n_steps, n_groups=n_groups, gd=gd, d_a=d_a, taps=taps),
        out_shape=(jax.ShapeDtypeStruct((t, d_a + d_b), BF16),
                   jax.ShapeDtypeStruct((n_dec, n_steps * d_a), F32),
                   jax.ShapeDtypeStruct((n_dec, taps - 1, d_b), F32)),
        grid=(n_steps,),
        in_specs=[smem, smem] + col_specs(lambda j: rows.prompt_blocks + j)
        + [pl.BlockSpec((None, n_dec, taps - 1, d_b), lambda j: (layer, 0, 0, 0)),
           const((1, d_a)), const((1, d_a)), const((taps, d_b)), pl.BlockSpec(memory_space=pl.ANY)],
        out_specs=(pl.BlockSpec((ROW_BLOCK, d_a + d_b), lambda j: (rows.prompt_blocks + j, 0)),
                   pl.BlockSpec((n_dec, d_a), lambda j: (0, j)),
                   pl.BlockSpec((n_dec, taps - 1, d_b), lambda j: (0, 0, 0))),
        scratch_shapes=[pltpu.VMEM((n_steps, n_dec, d_a), F32), pltpu.VMEM((taps - 1, n_dec, d_b), F32)],
        input_output_aliases={11: 0},
        compiler_params=_params(1),
        name="even_mixer_decode",
    )(w_s[:, :n_steps, :n_steps].reshape(-1), b_s[:, :n_steps].reshape(-1), proj, proj, proj, proj, proj,
      sc_states, ln_g.reshape(1, d_a), ln_b.reshape(1, d_a), sc_w, y)
    return y, tail[:, SUBLANES - (taps - 1):, :], sc_new, v_new.reshape(n_dec, n_steps, d_a)


def _silu(x):
    return x * jax.nn.sigmoid(x)


def _mpre_prompt_kernel(x_ref, w_ref, b_ref, o_ref, tail_ref, carry_ref, *, tiles_per_seq, taps):
    j = pl.program_id(1)

    @pl.when(j % tiles_per_seq == 0)
    def _():
        carry_ref[...] = jnp.zeros_like(carry_ref)

    x = x_ref[...]
    row = lax.broadcasted_iota(jnp.int32, x.shape, 0)
    acc = x * w_ref[taps - 1:taps, :]
    for d in range(1, taps):
        shifted = pltpu.roll(x, d, axis=0)
        for r in range(d):
            src = SUBLANES - d + r
            shifted = jnp.where(row == r, carry_ref[src:src + 1, :], shifted)
        acc = acc + shifted * w_ref[taps - 1 - d:taps - d, :]
    o_ref[...] = _silu(acc + b_ref[...])
    last = x[x.shape[0] - SUBLANES:, :]
    carry_ref[...] = last
    tail_ref[...] = last


def _mpre_sample_kernel(x_ref, st_ref, w_ref, b_ref, o_ref, new_ref, car_ref, *, taps, first_new):
    t = pl.program_id(1)

    @pl.when(t == 0)
    def _():
        for k in range(taps - 1):
            car_ref[k] = st_ref[:, k, :]

    x = x_ref[...]
    acc = x * w_ref[taps - 1:taps, :]
    for k in range(taps - 1):
        acc = acc + car_ref[k] * w_ref[k:k + 1, :]
    o_ref[...] = _silu(acc + b_ref[...])
    for k in range(taps - 1):
        @pl.when(t == first_new + k)
        def _():
            new_ref[:, k, :] = x
    for k in range(taps - 2):
        car_ref[k] = car_ref[k + 1]
    car_ref[taps - 2] = x


def _mamba_pre(zx, rows, xbc_col0, conv_states, layer, conv_w, conv_b):
    taps, conv_dim = conv_w.shape
    bw = _pick_block(conv_dim, 2048, xbc_col0)
    assert taps - 1 <= SUBLANES
    ncb = conv_dim // bw
    cb0 = xbc_col0 // bw
    n_dec, n_steps = rows.n_dec, rows.dec_seq
    wspec = pl.BlockSpec((taps, bw), lambda c, j: (0, c))
    bspec = pl.BlockSpec((1, bw), lambda c, j: (0, c))
    rt = rows.tile_rows(512)
    tiles_per_seq = rows.seq // rt
    xp, tail = pl.pallas_call(
        functools.partial(_mpre_prompt_kernel, tiles_per_seq=tiles_per_seq, taps=taps),
        out_shape=(jax.ShapeDtypeStruct((rows.tp, conv_dim), F32),
                   jax.ShapeDtypeStruct((rows.n_prompt, SUBLANES, conv_dim), F32)),
        grid=(ncb, rows.tp // rt),
        in_specs=[pl.BlockSpec((rt, bw), lambda c, j: (j, cb0 + c)), wspec, bspec],
        out_specs=(pl.BlockSpec((rt, bw), lambda c, j: (j, c)),
                   pl.BlockSpec((None, SUBLANES, bw), lambda c, j: (j // tiles_per_seq, 0, c))),
        scratch_shapes=[pltpu.VMEM((SUBLANES, bw), F32)],
        compiler_params=_params(2),
        name="mamba_conv_prompt",
    )(zx, conv_w, conv_b.reshape(1, conv_dim))

    xs, new = pl.pallas_call(
        functools.partial(_mpre_sample_kernel, taps=taps, first_new=n_steps - (taps - 1)),
        out_shape=(jax.ShapeDtypeStruct((n_dec, n_steps * conv_dim), F32),
                   jax.ShapeDtypeStruct((n_dec, taps - 1, conv_dim), F32)),
        grid=(ncb, n_steps),
        in_specs=[pl.BlockSpec((ROW_BLOCK, bw), lambda c, j: (rows.prompt_blocks + j, cb0 + c)),
                  pl.BlockSpec((None, n_dec, taps - 1, bw), lambda c, j: (layer, 0, 0, c)), wspec, bspec],
        out_specs=(pl.BlockSpec((n_dec, bw), lambda c, j: (0, j * ncb + c)),
                   pl.BlockSpec((n_dec, taps - 1, bw), lambda c, j: (0, 0, c))),
        scratch_shapes=[pltpu.VMEM((taps - 1, n_dec, bw), F32)],
        compiler_params=_params(2),
        name="mamba_conv_decode",
    )(zx, conv_states, conv_w, conv_b.reshape(1, conv_dim))
    return xp, xs.reshape(n_dec * n_steps, conv_dim), tail[:, SUBLANES - (taps - 1):, :], new


def _copy_kernel(x_ref, o_ref):
    o_ref[...] = x_ref[...].astype(o_ref.dtype)


def _steps_to_batch_major(x, rows, col0, n_cols, bw):
    assert n_cols % bw == 0 and col0 % bw == 0
    nc = n_cols // bw
    out = pl.pallas_call(
        _copy_kernel,
        out_shape=jax.ShapeDtypeStruct((rows.n_dec, rows.dec_seq * n_cols), F32),
        grid=(rows.dec_seq, nc),
        in_specs=[pl.BlockSpec((ROW_BLOCK, bw), lambda s, c: (rows.prompt_blocks + s, col0 // bw + c))],
        out_specs=pl.BlockSpec((rows.n_dec, bw), lambda s, c: (0, s * nc + c)),
        compiler_params=_params(2),
        name="to_batch_major",
    )(x)
    return out.reshape(rows.n_dec * rows.dec_seq, n_cols)


def _batch_to_steps_major(y_bt, into, rows, bw):
    c = y_bt.shape[1]
    assert c % bw == 0
    nc = c // bw
    return pl.pallas_call(
        lambda x_ref, _, o_ref: _copy_kernel(x_ref, o_ref),
        out_shape=jax.ShapeDtypeStruct(into.shape, into.dtype),
        grid=(rows.dec_seq, nc),
        in_specs=[pl.BlockSpec((rows.n_dec, bw), lambda s, j: (0, s * nc + j)),
                  pl.BlockSpec(memory_space=pl.ANY)],
        out_specs=pl.BlockSpec((ROW_BLOCK, bw), lambda s, j: (rows.prompt_blocks + s, j)),
        input_output_aliases={1: 0},
        compiler_params=_params(2),
        name="to_steps_major",
    )(y_bt.reshape(rows.n_dec, rows.dec_seq * c), into)


LOG2E = 1.4426950408889634


def _cumsum_rows(a):
    row = lax.broadcasted_iota(jnp.int32, a.shape, 0)
    shift = 1
    while shift < a.shape[0]:
        a = a + jnp.where(row >= shift, pltpu.roll(a, shift, axis=0), 0.0)
        shift *= 2
    return a


def _gate_norm(y, x, z, dsk, ng):
    yg = (y + dsk * x) * _silu(z)
    return yg * lax.rsqrt(jnp.mean(yg * yg, axis=-1, keepdims=True) + EPS) * ng


def _ssd_prompt_kernel(x_ref, b_ref, c_ref, dt_ref, z_ref, dtb_ref, alog_ref, dsk_ref, ng_ref, *rest,
                       heads, hd, ns, groups):
    y_ref, s_ref, xst_ref, yb_ref = rest[-4:]
    L = ROW_BLOCK
    hpg = heads // groups
    gw = hpg * hd
    hpt = LANES // hd

    @pl.when(pl.program_id(1) == 0)
    def _():
        s_ref[...] = jnp.zeros_like(s_ref)

    dt = jax.nn.softplus(dt_ref[...] + dtb_ref[...])
    cum2 = _cumsum_rows(dt * (-jnp.exp(alog_ref[...]) * LOG2E))
    cum2_t = cum2.T
    dt_t = dt.T
    crow = cum2_t - jnp.log2(dt_t)
    ri = lax.broadcasted_iota(jnp.int32, (L, L), 0)
    li = lax.broadcasted_iota(jnp.int32, (L, L), 1)
    tri = ri >= li
    lane_head = li // hd

    for g in range(groups):
        bg = b_ref[:, g * ns:(g + 1) * ns].astype(BF16)
        cg = c_ref[:, g * ns:(g + 1) * ns].astype(BF16)
        cb = _dot_nt(cg, bg)
        s_g = s_ref[g * hpg:(g + 1) * hpg].reshape(gw, ns)
        cs = _dot_nt(cg, s_g.astype(BF16))
        for j in range(gw // LANES):
            lanes = slice(g * gw + j * LANES, g * gw + (j + 1) * LANES)
            xp = x_ref[:, lanes]
            ws, blocks, cum_e = [], [], None
            for k in range(hpt):
                h = g * hpg + j * hpt + k
                ccol = jnp.broadcast_to(cum2[:, h:h + 1], (L, L))
                ws.append(jnp.where(tri, cb * jnp.exp2(ccol - crow[h:h + 1, :]), 0.0).astype(BF16))
                blocks.append(jnp.where(lane_head == k, xp, 0.0).astype(BF16))
                cum_e = ccol if k == 0 else jnp.where(lane_head >= k, ccol, cum_e)
            intra = _dot(jnp.concatenate(ws, axis=1), jnp.concatenate(blocks, axis=0))
            yb_ref[:, j * LANES:(j + 1) * LANES] = intra + cs[:, j * LANES:(j + 1) * LANES] * jnp.exp2(cum_e)

        x_t = x_ref[:, g * gw:(g + 1) * gw].T
        for r in range(hpg):
            h = g * hpg + r
            tail = jnp.exp2(cum2_t[h:h + 1, L - 1:L] - cum2_t[h:h + 1, :]) * dt_t[h:h + 1, :]
            xst_ref[r * hd:(r + 1) * hd, :] = (x_t[r * hd:(r + 1) * hd, :] * tail).astype(BF16)
        upd = _dot(xst_ref[...], bg)
        for r in range(hpg):
            h = g * hpg + r
            dec = jnp.exp2(jnp.broadcast_to(cum2[L - 1:L, h:h + 1], (hd, ns)))
            s_ref[h] = s_g[r * hd:(r + 1) * hd, :] * dec + upd[r * hd:(r + 1) * hd, :]

        sl = slice(g * gw, (g + 1) * gw)
        y_ref[:, sl] = _gate_norm(yb_ref[...], x_ref[:, sl], z_ref[:, sl], dsk_ref[:, sl],
                                  ng_ref[:, sl]).astype(y_ref.dtype)


def _ssd_decode_kernel(x_ref, b_ref, c_ref, dt_ref, z_ref, s0_ref, dtb_ref, alog_ref, dsk_ref, ng_ref, *rest,
                       n_steps, heads, hd, ns, groups):
    y_ref, s_ref = rest[-2:]
    T = n_steps
    hpg = heads // groups
    gw = hpg * hd
    hpt = LANES // hd
    d_inner = heads * hd

    dt = jax.nn.softplus(dt_ref[...] + dtb_ref[...])
    cum2 = _cumsum_rows(dt * (-jnp.exp(alog_ref[...]) * LOG2E))
    lane_head = lax.broadcasted_iota(jnp.int32, (T, LANES), 1) // hd
    cum_tiles, dt_tiles = [], []
    for j in range(d_inner // LANES):
        ce = de = None
        for k in range(hpt):
            h = j * hpt + k
            cc = jnp.broadcast_to(cum2[:, h:h + 1], (T, LANES))
            dd = jnp.broadcast_to(dt[:, h:h + 1], (T, LANES))
            ce = cc if k == 0 else jnp.where(lane_head >= k, cc, ce)
            de = dd if k == 0 else jnp.where(lane_head >= k, dd, de)
        cum_tiles.append(ce)
        dt_tiles.append(de)
    cum_e = jnp.concatenate(cum_tiles, axis=1)
    dt_e = jnp.concatenate(dt_tiles, axis=1)

    x = x_ref[...]
    xdt = x * dt_e
    xs = x * (jnp.exp2(cum_e[T - 1:T, :] - cum_e) * dt_e)
    step = lax.broadcasted_iota(jnp.int32, (T, 1), 0)
    acc = jnp.zeros((T, d_inner), F32)
    for s in range(T):
        e = jnp.exp2(jnp.minimum(cum_e - cum_e[s:s + 1, :], 0.0)) * xdt[s:s + 1, :]
        parts = []
        for g in range(groups):
            cb = jnp.sum(c_ref[:, g * ns:(g + 1) * ns] * b_ref[s:s + 1, g * ns:(g + 1) * ns], axis=-1, keepdims=True)
            parts.append(e[:, g * gw:(g + 1) * gw] * jnp.where(step >= s, cb, 0.0))
        acc = acc + jnp.concatenate(parts, axis=1)

    ecum = jnp.exp2(cum_e)
    pad = jnp.zeros((2 * SUBLANES - T, ns), F32)
    for g in range(groups):
        sl = slice(g * gw, (g + 1) * gw)
        s_g = s0_ref[g * hpg:(g + 1) * hpg].reshape(gw, ns)
        c16 = jnp.concatenate([c_ref[:, g * ns:(g + 1) * ns], pad], axis=0).astype(BF16)
        cs = _dot_nt(c16, s_g.astype(BF16))[:T]
        upd = lax.dot_general(xs[:, sl], b_ref[:, g * ns:(g + 1) * ns], (((0,), (0,)), ((), ())),
                              preferred_element_type=F32)
        for r in range(hpg):
            h = g * hpg + r
            dec = jnp.exp2(jnp.broadcast_to(cum2[T - 1:T, h:h + 1], (hd, ns)))
            s_ref[h] = s_g[r * hd:(r + 1) * hd, :] * dec + upd[r * hd:(r + 1) * hd, :]
        y_ref[:, sl] = _gate_norm(acc[:, sl] + cs * ecum[:, sl], x[:, sl], z_ref[:, sl], dsk_ref[:, sl],
                                  ng_ref[:, sl])


def _ssd_consts(dt_bias, a_log, d_skip, norm_g, hd):
    heads, d_inner = a_log.shape[0], norm_g.shape[0]
    return (dt_bias.reshape(1, heads), a_log.reshape(1, heads), jnp.repeat(d_skip, hd).reshape(1, d_inner),
            norm_g.reshape(1, d_inner))


def _ssd_prompt(xbc, dtraw, zx, consts, rows, layer, n_layers, prev_state, *, heads, hd, ns, groups):
    d_inner, gn = heads * hd, groups * ns
    cps = rows.chunks_per_seq
    row = lambda b, c: b * cps + c
    const = lambda width: pl.BlockSpec((1, width), lambda b, c: (0, 0))
    extra = [] if prev_state is None else [prev_state]
    n_in = 9 + len(extra)
    return pl.pallas_call(
        functools.partial(_ssd_prompt_kernel, heads=heads, hd=hd, ns=ns, groups=groups),
        out_shape=(jax.ShapeDtypeStruct((rows.t, d_inner), BF16),
                   jax.ShapeDtypeStruct((n_layers, rows.n_prompt, heads, hd, ns), F32)),
        grid=(rows.n_prompt, cps),
        in_specs=[
            pl.BlockSpec((ROW_BLOCK, d_inner), lambda b, c: (row(b, c), 0)),
            pl.BlockSpec((ROW_BLOCK, gn), lambda b, c: (row(b, c), d_inner // gn)),
            pl.BlockSpec((ROW_BLOCK, gn), lambda b, c: (row(b, c), d_inner // gn + 1)),
            pl.BlockSpec((ROW_BLOCK, heads), lambda b, c: (row(b, c), 0)),
            pl.BlockSpec((ROW_BLOCK, d_inner), lambda b, c: (row(b, c), 0)),
            const(heads), const(heads), const(d_inner), const(d_inner),
        ] + [pl.BlockSpec(memory_space=pl.ANY)] * len(extra),
        out_specs=(pl.BlockSpec((ROW_BLOCK, d_inner), lambda b, c: (row(b, c), 0)),
                   pl.BlockSpec((None, None, heads, hd, ns), lambda b, c: (layer, b, 0, 0, 0))),
        scratch_shapes=[pltpu.VMEM((d_inner // groups, ROW_BLOCK), BF16),
                        pltpu.VMEM((ROW_BLOCK, d_inner // groups), F32)],
        input_output_aliases={} if prev_state is None else {n_in - 1: 1},
        compiler_params=_params(2),
        name="ssd_prompt",
    )(xbc, xbc, xbc, dtraw, zx, *consts, *extra)


def _ssd_decode(xbc, dtraw, z, state, consts, rows, layer, prev_state, *, heads, hd, ns, groups):
    d_inner, gn = heads * hd, groups * ns
    n = rows.dec_seq
    const = lambda width: pl.BlockSpec((1, width), lambda b: (0, 0))
    extra = [] if prev_state is None else [prev_state]
    n_in = 10 + len(extra)
    state_spec = pl.BlockSpec((None, None, heads, hd, ns), lambda b: (layer, b, 0, 0, 0))
    return pl.pallas_call(
        functools.partial(_ssd_decode_kernel, n_steps=n, heads=heads, hd=hd, ns=ns, groups=groups),
        out_shape=(jax.ShapeDtypeStruct((rows.ts, d_inner), F32),
                   jax.ShapeDtypeStruct(state.shape, F32)),
        grid=(rows.n_dec,),
        in_specs=[
            pl.BlockSpec((n, d_inner), lambda b: (b, 0)),
            pl.BlockSpec((n, gn), lambda b: (b, d_inner // gn)),
            pl.BlockSpec((n, gn), lambda b: (b, d_inner // gn + 1)),
            pl.BlockSpec((n, heads), lambda b: (b, 0)),
            pl.BlockSpec((n, d_inner), lambda b: (b, 0)),
            state_spec,
            const(heads), const(heads), const(d_inner), const(d_inner),
        ] + [pl.BlockSpec(memory_space=pl.ANY)] * len(extra),
        out_specs=(pl.BlockSpec((n, d_inner), lambda b: (b, 0)), state_spec),
        input_output_aliases={} if prev_state is None else {n_in - 1: 1},
        compiler_params=_params(1),
        name="ssd_decode",
    )(xbc, xbc, xbc, dtraw, z, state, *consts, *extra)


def kernel(x_prompt, x_sample, state_shortconv, state_mamba_conv, state_ssm, c_prompt, c_sample, w_ada, b_ada, g_mix, g_ffn, ev_w_in, ev_ln_g, ev_ln_b, ev_w_s, ev_b_s, ev_sc_w, ev_w_out, m_w_in, m_conv_w, m_conv_b, m_dt_bias, m_a_log, m_d, m_norm_g, m_w_out, ff_w1, ff_w2, g_final):
    n_prompt, seq, d = x_prompt.shape
    n_dec, dec_seq, _ = x_sample.shape
    rows = _Rows(n_prompt, seq, n_dec, dec_seq)
    depth = w_ada.shape[0]
    d_ff = ff_w1.shape[2]
    d_a = ev_ln_g.shape[1]
    d_b = ev_sc_w.shape[2]
    heads = m_a_log.shape[1]
    d_inner = m_norm_g.shape[1]
    conv_dim = m_conv_w.shape[2]
    ns = state_ssm.shape[-1]
    groups = (conv_dim - d_inner) // (2 * ns)
    gn = groups * ns
    assert d_inner % gn == 0 and m_w_in.shape[2] == 2 * d_inner + 2 * gn + heads

    bm = rows.tile_rows(512)
    bm_k = rows.tile_rows(1024)
    lane_block = _pick_block(d_inner, 2048)

    x = jnp.concatenate([x_prompt.reshape(rows.tp, d),
                         jnp.swapaxes(x_sample, 0, 1).reshape(rows.ts, d)], axis=0)
    mods = _ada_all(c_prompt, c_sample, w_ada, b_ada)

    sc_p, sc_s, v_s, mc_p, mc_s = [], [], [], [], []
    ssm_p = ssm_s = None
    n_odd = state_ssm.shape[0]
    hd = d_inner // heads
    for i in range(depth):
        hm = _norm_mod(x, g_mix[i], rows, mods, i, shift_col=0, scale_col=1, name="norm_mix")
        if i % 2 == 0:
            e = i // 2
            n_proj = ev_w_in.shape[2]
            proj = _matmul_stationary(hm, ev_w_in, e, rows, n_cols=n_proj, bm=bm_k, bn_cap=1024, name="even_in")
            ymix, scp, scs, vs = _even_mixer(proj, rows, state_shortconv, e, ev_ln_g[e], ev_ln_b[e],
                                             ev_w_s[e], ev_b_s[e], ev_sc_w[e])
            sc_p.append(scp)
            sc_s.append(scs)
            v_s.append(vs)
            x = _matmul_stationary(ymix, ev_w_out, e, rows, n_cols=d, bm=bm, bn_cap=1024, epi="resid", res=x,
                                   mods=mods, mod_layer=i, gate_col=2 * d, name="even_out")
        else:
            o = i // 2
            n_zx = 2 * d_inner + 2 * gn
            zx = _matmul_stationary(hm, m_w_in, o, rows, n_cols=n_zx, bm=bm_k, bn_cap=1024, name="mamba_in")
            dtraw = _matmul(hm, m_w_in, o, rows, n_cols=heads, col0=n_zx, bm=bm, bn_cap=128, name="mamba_dt")
            xp, xs, mcp, mcs = _mamba_pre(zx, rows, d_inner, state_mamba_conv, o, m_conv_w[o], m_conv_b[o])
            mc_p.append(mcp)
            mc_s.append(mcs)
            consts = _ssd_consts(m_dt_bias[o], m_a_log[o], m_d[o], m_norm_g[o], hd)
            dims = dict(heads=heads, hd=hd, ns=ns, groups=groups)
            ymix, ssm_p = _ssd_prompt(xp, dtraw, zx, consts, rows, o, n_odd, ssm_p, **dims)
            z_s = _steps_to_batch_major(zx, rows, 0, d_inner, lane_block)
            dt_s = _steps_to_batch_major(dtraw, rows, 0, heads, heads)
            ys, ssm_s = _ssd_decode(xs, dt_s, z_s, state_ssm, consts, rows, o, ssm_s, **dims)
            ymix = _batch_to_steps_major(ys, ymix, rows, lane_block)
            x = _matmul_stationary(ymix, m_w_out, o, rows, n_cols=d, bm=bm, bn_cap=512, epi="resid", res=x,
                                   mods=mods, mod_layer=i, gate_col=2 * d, name="mamba_out")
        hf = _norm_mod(x, g_ffn[i], rows, mods, i, shift_col=3, scale_col=4, name="norm_ffn")
        a2 = _matmul_stationary(hf, ff_w1, i, rows, n_cols=d_ff, bm=bm_k, bn_cap=1024, epi="relu2",
                                out_dtype=BF16, name="ffn_up")
        x = _matmul(a2, ff_w2, i, rows, n_cols=d, bm=bm_k, bn_cap=1024, bk=d_ff // 8, epi="resid", res=x,
                    mods=mods, mod_layer=i, gate_col=5 * d, name="ffn_down")

    y_prompt, y_sample = _norm_final(x, g_final, rows)
    return (y_prompt, y_sample, jnp.stack(sc_p), jnp.stack(sc_s), jnp.stack(v_s), jnp.stack(mc_p),
            jnp.stack(mc_s), ssm_p, ssm_s)
```
